```python
import math
import jax, jax.numpy as jnp
from jax import lax
import numpy as np

D_MODEL = 2048
BATCH = 4
SEQ = 2048
DEPTH = 4

CHUNK = 64
EPS = 1e-6
N_EVEN = (DEPTH + 1) // 2
N_ODD = DEPTH // 2

GM_BLOCK = 128
GM_GROUPS = 8
GM_WIDTH = D_MODEL
GM_GROUP_DIM = GM_WIDTH // GM_GROUPS
SSM_D_INNER = D_MODEL
SSM_HEAD_DIM = 64
SSM_HEADS = SSM_D_INNER // SSM_HEAD_DIM
SSM_GROUPS = 4
SSM_HEADS_PER_GROUP = SSM_HEADS // SSM_GROUPS
SSM_STATE = 128
SSM_CONV = 4
SSM_CHUNK = CHUNK
SSM_CONV_DIM = SSM_D_INNER + 2 * SSM_GROUPS * SSM_STATE
EVEN_IN = 2 * GM_WIDTH + SSM_D_INNER + SSM_CONV_DIM + SSM_HEADS
EVEN_MIX = GM_WIDTH + SSM_D_INNER
MLA_HEADS = 16
MLA_Q_RANK = 512
MLA_KV_RANK = 512
MLA_NOPE = 128
MLA_ROPE = 64
MLA_V = 128
MLA_QK = MLA_NOPE + MLA_ROPE
ODD_IN = MLA_Q_RANK + MLA_KV_RANK + MLA_ROPE
ATTN_BLOCK = 128
ROPE_THETA = 10000.0
MAX_OFFSET_CHUNKS = 64
D_FF = 5632
FFN_CONV = 3

kernel_name = "hybrid_gmlp_ssd_mla_convffn"


def rmsnorm(x, w):
    xf = x.astype(jnp.float32)
    y = xf * lax.rsqrt(jnp.mean(xf * xf, -1, keepdims=True) + EPS)
    return (y * w.astype(jnp.float32)).astype(x.dtype)


def causal_dwconv(x, w, b):
    k = w.shape[0]
    c = x.shape[-1]
    y = lax.conv_general_dilated(
        x, w[:, None, :].astype(x.dtype), window_strides=(1,),
        padding=[(k - 1, 0)], dimension_numbers=("NWC", "WIO", "NWC"),
        feature_group_count=c)
    return y + b.astype(x.dtype)


def gmlp_sgu(u, v, ln_g, ln_b, w_s, b_s):
    bsz, s, _ = u.shape
    nb = s // GM_BLOCK
    vf = v.reshape(bsz, nb, GM_BLOCK, GM_GROUPS, GM_GROUP_DIM).astype(jnp.float32)
    mu = jnp.mean(vf, -1, keepdims=True)
    var = jnp.mean(jnp.square(vf - mu), -1, keepdims=True)
    vn = ((vf - mu) * lax.rsqrt(var + EPS) * ln_g + ln_b).astype(u.dtype)
    chunk_id = jnp.arange(GM_BLOCK) // CHUNK
    mask = chunk_id[:, None] >= chunk_id[None, :]
    ws = jnp.where(mask, w_s, jnp.zeros((), w_s.dtype)).astype(u.dtype)
    gate = jnp.einsum("gij,bnjgc->bnigc", ws, vn) + b_s.T[:, :, None].astype(u.dtype)
    return u * gate.reshape(bsz, s, GM_WIDTH)


def segsum(a):
    t = a.shape[-1]
    cs = jnp.cumsum(a, -1)
    d = cs[..., :, None] - cs[..., None, :]
    mask = jnp.tril(jnp.ones((t, t), dtype=bool))
    return jnp.where(mask, d, -jnp.inf)


def ssd_scan(x, dt, a, b, c):
    bsz, s = x.shape[:2]
    nc = s // SSM_CHUNK
    G, E, P, N, L = SSM_GROUPS, SSM_HEADS_PER_GROUP, SSM_HEAD_DIM, SSM_STATE, SSM_CHUNK
    xd = (x * dt[..., None]).reshape(bsz, nc, L, G, E, P)
    da = jnp.moveaxis((dt * a).reshape(bsz, nc, L, G, E), 2, -1)
    bc = b.reshape(bsz, nc, L, G, N)
    cc = c.reshape(bsz, nc, L, G, N)
    a_cum = jnp.cumsum(da, -1)
    decay = jnp.exp(segsum(da))
    cb = jnp.einsum("bclgn,bcsgn->bcgls", cc, bc)
    y_diag = jnp.einsum("bcgls,bcgels,bcsgep->bclgep", cb, decay, xd)
    decay_states = jnp.exp(a_cum[..., -1:] - a_cum)
    states = jnp.einsum("bclgn,bcgel,bclgep->bcgepn", bc, decay_states, xd)
    chunk_decay = jnp.exp(a_cum[..., -1])

    def step(h, inp):
        s_c, d_c = inp
        return d_c[..., None, None] * h + s_c, h

    h0 = jnp.zeros((bsz, G, E, P, N), jnp.float32)
    _, prev = lax.scan(step, h0, (jnp.moveaxis(states, 1, 0), jnp.moveaxis(chunk_decay, 1, 0)))
    prev = jnp.moveaxis(prev, 0, 1)
    y_off = jnp.einsum("bclgn,bcgepn,bcgel->bclgep", cc, prev, jnp.exp(a_cum))
    return (y_diag + y_off).reshape(bsz, s, SSM_HEADS, P)


def mamba_branch(z, xbc, dt_raw, conv_w, conv_b, dt_bias, a_log, d_skip, norm_w):
    bsz, s, _ = z.shape
    xbc = jax.nn.silu(causal_dwconv(xbc, conv_w, conv_b)).astype(jnp.float32)
    xs, bs, cs = jnp.split(xbc, [SSM_D_INNER, SSM_D_INNER + SSM_GROUPS * SSM_STATE], axis=-1)
    dt = jax.nn.softplus(dt_raw.astype(jnp.float32) + dt_bias.astype(jnp.float32))
    a = -jnp.exp(a_log.astype(jnp.float32))
    xh = xs.reshape(bsz, s, SSM_HEADS, SSM_HEAD_DIM)
    y = ssd_scan(xh, dt, a,
                 bs.reshape(bsz, s, SSM_GROUPS, SSM_STATE),
                 cs.reshape(bsz, s, SSM_GROUPS, SSM_STATE))
    y = y + d_skip.astype(jnp.float32)[:, None] * xh
    y = y.reshape(bsz, s, SSM_D_INNER) * jax.nn.silu(z.astype(jnp.float32))
    y = y.reshape(bsz, s, SSM_GROUPS, SSM_D_INNER // SSM_GROUPS)
    y = y * lax.rsqrt(jnp.mean(y * y, -1, keepdims=True) + EPS)
    y = y.reshape(bsz, s, SSM_D_INNER) * norm_w.astype(jnp.float32)
    return y.astype(z.dtype)


def even_mixer(h, w_in, gm_ln_g, gm_ln_b, gm_ws, gm_bs, conv_w, conv_b,
               dt_bias, a_log, d_skip, ssm_norm_w, w_out):
    proj = h @ w_in
    o1 = GM_WIDTH
    o2 = 2 * GM_WIDTH
    o3 = o2 + SSM_D_INNER
    o4 = o3 + SSM_CONV_DIM
    u, v, z, xbc, dt_raw = jnp.split(proj, [o1, o2, o3, o4], axis=-1)
    ya = gmlp_sgu(jax.nn.gelu(u), jax.nn.gelu(v), gm_ln_g, gm_ln_b, gm_ws, gm_bs)
    yb = mamba_branch(z, xbc, dt_raw, conv_w, conv_b, dt_bias, a_log, d_skip, ssm_norm_w)
    return jnp.concatenate([ya, yb], axis=-1) @ w_out


def rope(x, cos, sin):
    half = x.shape[-1] // 2
    x1, x2 = x[..., :half], x[..., half:]
    return jnp.concatenate([x1 * cos - x2 * sin, x1 * sin + x2 * cos], axis=-1)


def mla_mixer(h, cos, sin, w_in, q_norm_w, kv_norm_w, w_uq, w_ukv, w_o):
    bsz, s, _ = h.shape
    proj = h @ w_in
    cq, ckv, kr = jnp.split(proj, [MLA_Q_RANK, MLA_Q_RANK + MLA_KV_RANK], axis=-1)
    cq = rmsnorm(cq, q_norm_w)
    ckv = rmsnorm(ckv, kv_norm_w)
    q = (cq @ w_uq).reshape(bsz, s, MLA_HEADS, MLA_QK)
    q_nope = q[..., :MLA_NOPE]
    q_pe = rope(q[..., MLA_NOPE:], cos[:, :, None, :], sin[:, :, None, :])
    kv = (ckv @ w_ukv).reshape(bsz, s, MLA_HEADS, MLA_NOPE + MLA_V)
    k_nope, v = kv[..., :MLA_NOPE], kv[..., MLA_NOPE:]
    k_pe = rope(kr, cos, sin)
    scale = MLA_QK ** -0.5
    outs = []
    for i in range(s // ATTN_BLOCK):
        q0 = i * ATTN_BLOCK
        kend = q0 + ATTN_BLOCK
        sc = (jnp.einsum("bqhd,bkhd->bhqk", q_nope[:, q0:kend], k_nope[:, :kend])
              + jnp.einsum("bqhr,bkr->bhqk", q_pe[:, q0:kend], k_pe[:, :kend]))
        sc = sc.astype(jnp.float32) * scale
        qc = (q0 + jnp.arange(ATTN_BLOCK)) // CHUNK
        kc = jnp.arange(kend) // CHUNK
        sc = jnp.where(kc[None, :] <= qc[:, None], sc, -jnp.inf)
        p = jax.nn.softmax(sc, axis=-1).astype(v.dtype)
        outs.append(jnp.einsum("bhqk,bkhd->bqhd", p, v[:, :kend]))
    o = jnp.concatenate(outs, axis=1).reshape(bsz, s, MLA_HEADS * MLA_V)
    return o @ w_o


def conv_ffn(h, w_up, conv_w, conv_b, w_down):
    up = h @ w_up
    g, val = up[..., :D_FF], up[..., D_FF:]
    g = causal_dwconv(g, conv_w, conv_b)
    return (jax.nn.gelu(g) * val) @ w_down


def setup_inputs(seed: int = 0) -> dict:
    key = jax.random.key(seed)
    ks = iter(jax.random.split(key, 40))

    def nrm(shape, scale):
        return jax.random.normal(next(ks), shape, jnp.float32) * scale

    def gain(shape):
        return 1.0 + nrm(shape, 0.02)

    x = jax.random.normal(next(ks), (BATCH, SEQ, D_MODEL), jnp.float32)
    offset = jax.random.randint(next(ks), (BATCH,), 0, MAX_OFFSET_CHUNKS) * CHUNK
    positions = (offset[:, None] + jnp.arange(SEQ)[None, :]).astype(jnp.int32)

    dt = jnp.exp(jax.random.uniform(next(ks), (N_EVEN, SSM_HEADS), jnp.float32)
                 * (math.log(0.1) - math.log(0.001)) + math.log(0.001))
    dt_bias = dt + jnp.log(-jnp.expm1(-dt))
    a_log = jnp.log(jax.random.uniform(next(ks), (N_EVEN, SSM_HEADS), jnp.float32, 1.0, 16.0))

    return {
        "x": x,
        "positions": positions,
        "norm_mix": gain((DEPTH, D_MODEL)),
        "norm_ffn": gain((DEPTH, D_MODEL)),
        "norm_final": gain((D_MODEL,)),
        "ev_w_in": nrm((N_EVEN, D_MODEL, EVEN_IN), D_MODEL ** -0.5),
        "ev_gm_ln_g": gain((N_EVEN, GM_GROUPS, GM_GROUP_DIM)),
        "ev_gm_ln_b": nrm((N_EVEN, GM_GROUPS, GM_GROUP_DIM), 0.02),
        "ev_gm_ws": nrm((N_EVEN, GM_GROUPS, GM_BLOCK, GM_BLOCK), GM_BLOCK ** -0.5),
        "ev_gm_bs": gain((N_EVEN, GM_GROUPS, GM_BLOCK)),
        "ev_conv_w": nrm((N_EVEN, SSM_CONV, SSM_CONV_DIM), SSM_CONV ** -0.5),
        "ev_conv_b": nrm((N_EVEN, SSM_CONV_DIM), 0.02),
        "ev_dt_bias": dt_bias,
        "ev_a_log": a_log,
        "ev_d_skip": gain((N_EVEN, SSM_HEADS)),
        "ev_ssm_norm_w": gain((N_EVEN, SSM_D_INNER)),
        "ev_w_out": nrm((N_EVEN, EVEN_MIX, D_MODEL), EVEN_MIX ** -0.5),
        "od_w_in": nrm((N_ODD, D_MODEL, ODD_IN), D_MODEL ** -0.5),
        "od_q_norm": gain((N_ODD, MLA_Q_RANK)),
        "od_kv_norm": gain((N_ODD, MLA_KV_RANK)),
        "od_w_uq": nrm((N_ODD, MLA_Q_RANK, MLA_HEADS * MLA_QK), MLA_Q_RANK ** -0.5),
        "od_w_ukv": nrm((N_ODD, MLA_KV_RANK, MLA_HEADS * (MLA_NOPE + MLA_V)), MLA_KV_RANK ** -0.5),
        "od_w_o": nrm((N_ODD, MLA_HEADS * MLA_V, D_MODEL), (MLA_HEADS * MLA_V) ** -0.5),
        "ff_w_up": nrm((DEPTH, D_MODEL, 2 * D_FF), D_MODEL ** -0.5),
        "ff_conv_w": nrm((DEPTH, FFN_CONV, D_FF), FFN_CONV ** -0.5),
        "ff_conv_b": nrm((DEPTH, D_FF), 0.02),
        "ff_w_down": nrm((DEPTH, D_FF, D_MODEL), D_FF ** -0.5),
    }


def reference(x, positions, norm_mix, norm_ffn, norm_final,
              ev_w_in, ev_gm_ln_g, ev_gm_ln_b, ev_gm_ws, ev_gm_bs,
              ev_conv_w, ev_conv_b, ev_dt_bias, ev_a_log, ev_d_skip, ev_ssm_norm_w, ev_w_out,
              od_w_in, od_q_norm, od_kv_norm, od_w_uq, od_w_ukv, od_w_o,
              ff_w_up, ff_conv_w, ff_conv_b, ff_w_down):
    inv_freq = ROPE_THETA ** (-jnp.arange(0, MLA_ROPE, 2, dtype=jnp.float32) / MLA_ROPE)
    ang = positions.astype(jnp.float32)[..., None] * inv_freq
    cos = jnp.cos(ang).astype(x.dtype)
    sin = jnp.sin(ang).astype(x.dtype)

    h = x
    for layer in range(DEPTH):
        j = layer // 2
        hn = rmsnorm(h, norm_mix[layer])
        if layer % 2 == 0:
            mix = even_mixer(hn, ev_w_in[j], ev_gm_ln_g[j], ev_gm_ln_b[j], ev_gm_ws[j], ev_gm_bs[j],
                             ev_conv_w[j], ev_conv_b[j], ev_dt_bias[j], ev_a_log[j], ev_d_skip[j],
                             ev_ssm_norm_w[j], ev_w_out[j])
        else:
            mix = mla_mixer(hn, cos, sin, od_w_in[j], od_q_norm[j], od_kv_norm[j],
                            od_w_uq[j], od_w_ukv[j], od_w_o[j])
        h = h + mix
        h = h + conv_ffn(rmsnorm(h, norm_ffn[layer]), ff_w_up[layer], ff_conv_w[layer],
                         ff_conv_b[layer], ff_w_down[layer])
    return rmsnorm(h, norm_final)
```

```python
import functools
import math

import jax
import jax.numpy as jnp
from jax import lax
from jax.experimental import pallas as pl
from jax.experimental.pallas import tpu as pltpu

F32 = jnp.float32
BF16 = jnp.bfloat16

EPS = 1e-6
LANES = 128
SUBLANES = 8
VMEM_LIMIT = 56 * 1024 * 1024

CHUNK = 64
GM_BLOCK = 128
GM_GROUPS = 8
SSM_HEAD_DIM = 64
SSM_GROUPS = 4
SSM_STATE = 128
SSM_CONV = 4
MLA_HEADS = 16
MLA_NOPE = 128
MLA_ROPE = 64
MLA_V = 128
MLA_RANK = 512
ROPE_THETA = 10000.0
FFN_CONV = 3

HIGHEST = lax.Precision.HIGHEST


def _params(sem):
    return pltpu.CompilerParams(dimension_semantics=sem, vmem_limit_bytes=VMEM_LIMIT)


def _dot(a, b, precision=None):
    return jnp.dot(a, b, preferred_element_type=F32, precision=precision)


def _dot_nt(a, b):
    return lax.dot_general(a, b, (((1,), (1,)), ((), ())), preferred_element_type=F32)


def _dot_tn(a, b):
    return lax.dot_general(a, b, (((0,), (0,)), ((), ())), preferred_element_type=F32)


def _rms(x, w):
    ms = jnp.mean(x * x, axis=-1, keepdims=True)
    return x * lax.rsqrt(ms + EPS) * w


def _gelu(x):
    return 0.5 * x * (1.0 + jnp.tanh(0.7978845608028654 * (x + 0.044715 * (x * x * x))))


def _silu(x):
    return x * (1.0 / (1.0 + jnp.exp(-x)))


def _tile(n, pref):
    if n <= pref:
        return n
    t = pref
    while t >= LANES:
        if n % t == 0:
            return t
        t -= LANES
    return n


def _rope_table_kernel(pos_ref, freq_ref, o_ref):
    ang = pos_ref[...] * freq_ref[...]
    lane = lax.broadcasted_iota(jnp.int32, ang.shape, 1)
    o_ref[...] = jnp.where(lane < 2 * (MLA_ROPE // 2), jnp.cos(ang), jnp.sin(ang))


def rope_table(pos_b, freq_row):
    m = pos_b.shape[0]
    tm = _tile(m, 1024)
    return pl.pallas_call(
        _rope_table_kernel,
        grid=(m // tm,),
        in_specs=[pl.BlockSpec((tm, LANES), lambda i: (i, 0)),
                  pl.BlockSpec((1, LANES), lambda i: (0, 0))],
        out_specs=pl.BlockSpec((tm, LANES), lambda i: (i, 0)),
        out_shape=jax.ShapeDtypeStruct((m, LANES), F32),
        compiler_params=_params(("parallel",)),
        name="rope_table",
    )(pos_b, freq_row)


def _even_inproj_kernel(x_ref, nw_ref, w_ref, wdt_ref, o_ref, dt_ref, xn_ref):
    @pl.when(pl.program_id(1) == 0)
    def _():
        xn = _rms(x_ref[...], nw_ref[...]).astype(BF16)
        xn_ref[...] = xn
        dt_ref[...] = _dot(xn, wdt_ref[...])

    o_ref[...] = _dot(xn_ref[...], w_ref[...])


def even_inproj(h, nw, w, wdt):
    m, k = h.shape
    n = w.shape[1]
    tm = _tile(m, 512)
    tn = _tile(n, 1024)
    return pl.pallas_call(
        _even_inproj_kernel,
        grid=(m // tm, n // tn),
        in_specs=[pl.BlockSpec((tm, k), lambda i, j: (i, 0)),
                  pl.BlockSpec((1, k), lambda i, j: (0, 0)),
                  pl.BlockSpec((k, tn), lambda i, j: (0, j)),
                  pl.BlockSpec((k, LANES), lambda i, j: (0, 0))],
        out_specs=[pl.BlockSpec((tm, tn), lambda i, j: (i, j)),
                   pl.BlockSpec((tm, LANES), lambda i, j: (i, 0))],
        out_shape=[jax.ShapeDtypeStruct((m, n), F32),
                   jax.ShapeDtypeStruct((m, LANES), F32)],
        scratch_shapes=[pltpu.VMEM((tm, k), BF16)],
        compiler_params=_params(("parallel", "arbitrary")),
        name="even_inproj",
    )(h, nw, w, wdt)


def _sgu_kernel(u_ref, v_ref, g_ref, b_ref, ws_ref, bs_ref, o_ref, *, nblk, gd):
    row = lax.broadcasted_iota(jnp.int32, (GM_BLOCK, GM_BLOCK), 0) // CHUNK
    col = lax.broadcasted_iota(jnp.int32, (GM_BLOCK, GM_BLOCK), 1) // CHUNK
    keep = row >= col
    for g in range(GM_GROUPS):
        cols = slice(g * gd, (g + 1) * gd)
        wsg = jnp.where(keep, ws_ref[g], 0.0).astype(BF16)
        ln_g = g_ref[:, cols]
        ln_b = b_ref[:, cols]
        bsg = bs_ref[:, g:g + 1]
        for nb in range(nblk):
            rows = slice(nb * GM_BLOCK, (nb + 1) * GM_BLOCK)
            v = _gelu(v_ref[rows, cols])
            mu = jnp.mean(v, axis=-1, keepdims=True)
            d = v - mu
            var = jnp.mean(d * d, axis=-1, keepdims=True)
            vn = (d * lax.rsqrt(var + EPS) * ln_g + ln_b).astype(BF16)
            gate = _dot(wsg, vn) + bsg
            o_ref[rows, cols] = (_gelu(u_ref[rows, cols]) * gate).astype(o_ref.dtype)


def sgu(proj, ln_g, ln_b, ws, bs_t, width):
    m = proj.shape[0]
    gd = width // GM_GROUPS
    nblk = 2 if m % (2 * GM_BLOCK) == 0 else 1
    t = nblk * GM_BLOCK
    return pl.pallas_call(
        functools.partial(_sgu_kernel, nblk=nblk, gd=gd),
        grid=(m // t,),
        in_specs=[pl.BlockSpec((t, width), lambda i: (i, 0)),
                  pl.BlockSpec((t, width), lambda i: (i, 1)),
                  pl.BlockSpec((1, width), lambda i: (0, 0)),
                  pl.BlockSpec((1, width), lambda i: (0, 0)),
                  pl.BlockSpec((GM_GROUPS, GM_BLOCK, GM_BLOCK), lambda i: (0, 0, 0)),
                  pl.BlockSpec((GM_BLOCK, GM_GROUPS), lambda i: (0, 0))],
        out_specs=pl.BlockSpec((t, width), lambda i: (i, 0)),
        out_shape=jax.ShapeDtypeStruct((m, width), BF16),
        compiler_params=_params(("parallel",)),
        name="sgu",
    )(proj, proj, ln_g, ln_b, ws, bs_t)


def _ssd_kernel(z_ref, xbc_ref, dt_ref, cw_ref, cb_ref, dtb_ref, alog_ref, dsk_ref, nw_ref,
                e_ref, o_ref, halo_ref, xc_ref, xd_ref, da_ref, y_ref, h_ref,
                *, t_rows, d_inner):
    L = CHUNK
    P = SSM_HEAD_DIM
    N = SSM_STATE
    G = SSM_GROUPS
    gw = d_inner // G
    heads_per_group = gw // P
    conv_dim = d_inner + 2 * G * N
    cchunk = 512

    @pl.when(pl.program_id(1) == 0)
    def _():
        halo_ref[...] = jnp.zeros_like(halo_ref)
        h_ref[...] = jnp.zeros_like(h_ref)

    for c0 in range(0, conv_dim, cchunk):
        cs_ = slice(c0, c0 + cchunk)
        xext = jnp.concatenate([halo_ref[:, cs_], xbc_ref[:, cs_]], axis=0)
        acc = cb_ref[:, cs_] + cw_ref[0:1, cs_] * xext[SUBLANES - 3:SUBLANES - 3 + t_rows]
        for k in range(1, SSM_CONV):
            off = SUBLANES - (SSM_CONV - 1) + k
            acc = acc + cw_ref[k:k + 1, cs_] * xext[off:off + t_rows]
        xc_ref[:, cs_] = _silu(acc)
        halo_ref[:, cs_] = xbc_ref[t_rows - SUBLANES:t_rows, cs_]

    dtr = dt_ref[...] + dtb_ref[...]
    dt = jnp.maximum(dtr, 0.0) + jnp.log1p(jnp.exp(-jnp.abs(dtr)))
    a = -jnp.exp(alog_ref[...])
    da_ref[...] = dt * a
    e_mat = e_ref[...]
    xd_ref[...] = xc_ref[:, 0:d_inner] * _dot(dt, e_mat, HIGHEST)

    ri = lax.broadcasted_iota(jnp.int32, (L, L), 0)
    ci = lax.broadcasted_iota(jnp.int32, (L, L), 1)
    lower = ri >= ci
    tril = jnp.where(lower, 1.0, 0.0).astype(F32)

    def chunk(c, carry):
        r0 = pl.multiple_of(c * L, L)
        rows = pl.ds(r0, L)
        da_c = da_ref[rows, :]
        cum = _dot(tril, da_c, HIGHEST)
        cum_last = cum[L - 1:L, :]
        stack = jnp.concatenate([cum, jnp.exp(cum_last - cum), jnp.exp(cum)], axis=0)
        ex = _dot(stack, e_mat, HIGHEST)
        cum_t = cum.T
        for g in range(G):
            gc = slice(g * gw, (g + 1) * gw)
            b_g = xc_ref[rows, d_inner + g * N:d_inner + (g + 1) * N].astype(BF16)
            c_g = xc_ref[rows, d_inner + G * N + g * N:d_inner + G * N + (g + 1) * N].astype(BF16)
            cb = _dot_nt(c_g, b_g)
            xd_g = xd_ref[rows, gc]
            cum_x = ex[0:L, gc]
            dst_x = ex[L:2 * L, gc]
            eac_x = ex[2 * L:3 * L, gc]
            h_g = h_ref[g]
            y_off = _dot(c_g, h_g.astype(BF16)) * eac_x
            st = _dot_tn(b_g, (xd_g * dst_x).astype(BF16))
            h_ref[g] = h_g * eac_x[L - 1:L, :] + st
            xd_b = xd_g.astype(BF16)
            parts = []
            for e in range(heads_per_group):
                hd = g * heads_per_group + e
                col_b = cum_x[:, e * P:(e + 1) * P]
                row_b = cum_t[hd:hd + 1, :]
                dec = jnp.exp(jnp.where(lower, col_b - row_b, -jnp.inf))
                mm = (cb * dec).astype(BF16)
                parts.append(_dot(mm, xd_b[:, e * P:(e + 1) * P]))
            y_ref[rows, gc] = jnp.concatenate(parts, axis=1) + y_off
        return carry

    lax.fori_loop(0, t_rows // L, chunk, 0)

    for g in range(G):
        gc = slice(g * gw, (g + 1) * gw)
        y = y_ref[:, gc] + dsk_ref[:, gc] * xc_ref[:, gc]
        y = y * _silu(z_ref[:, gc])
        o_ref[:, gc] = _rms(y, nw_ref[:, gc]).astype(o_ref.dtype)


def ssd(proj, dt_raw, conv_w, conv_b, dt_bias, a_log, d_skip_x, norm_w, e_mat, seq, d_inner):
    m = proj.shape[0]
    conv_dim = d_inner + 2 * SSM_GROUPS * SSM_STATE
    t = _tile(seq, 256)
    spb = seq // t
    nb = m // seq
    z_blk = (2 * d_inner) // d_inner
    xbc_blk = (3 * d_inner) // conv_dim
    assert 3 * d_inner == xbc_blk * conv_dim
    kern = functools.partial(_ssd_kernel, t_rows=t, d_inner=d_inner)
    row = lambda b, s: (b * spb + s, 0)
    const = lambda b, s: (0, 0)
    return pl.pallas_call(
        kern,
        grid=(nb, spb),
        in_specs=[pl.BlockSpec((t, d_inner), lambda b, s: (b * spb + s, z_blk)),
                  pl.BlockSpec((t, conv_dim), lambda b, s: (b * spb + s, xbc_blk)),
                  pl.BlockSpec((t, LANES), row),
                  pl.BlockSpec((SSM_CONV, conv_dim), const),
                  pl.BlockSpec((1, conv_dim), const),
                  pl.BlockSpec((1, LANES), const),
                  pl.BlockSpec((1, LANES), const),
                  pl.BlockSpec((1, d_inner), const),
                  pl.BlockSpec((1, d_inner), const),
                  pl.BlockSpec((LANES, d_inner), const)],
        out_specs=pl.BlockSpec((t, d_inner), row),
        out_shape=jax.ShapeDtypeStruct((m, d_inner), BF16),
        scratch_shapes=[pltpu.VMEM((SUBLANES, conv_dim), F32),
                        pltpu.VMEM((t, conv_dim), F32),
                        pltpu.VMEM((t, d_inner), F32),
                        pltpu.VMEM((t, LANES), F32),
                        pltpu.VMEM((t, d_inner), F32),
                        pltpu.VMEM((SSM_GROUPS, SSM_STATE, d_inner // SSM_GROUPS), F32)],
        compiler_params=_params(("arbitrary", "arbitrary")),
        name="ssd",
    )(proj, proj, dt_raw, conv_w, conv_b, dt_bias, a_log, d_skip_x, norm_w, e_mat)


def _mm_kernel(x_ref, w_ref, o_ref):
    o_ref[...] = _dot(x_ref[...], w_ref[...]).astype(o_ref.dtype)


def _mm_res_kernel(x_ref, w_ref, r_ref, o_ref):
    o_ref[...] = r_ref[...] + _dot(x_ref[...], w_ref[...])


def matmul(x, w, res=None, out_dtype=F32, tm_pref=512, tn_pref=512):
    m, k = x.shape
    n = w.shape[1]
    tm = _tile(m, tm_pref)
    tn = _tile(n, tn_pref)
    in_specs = [pl.BlockSpec((tm, k), lambda i, j: (i, 0)),
                pl.BlockSpec((k, tn), lambda i, j: (0, j))]
    args = [x, w]
    kern = _mm_kernel
    if res is not None:
        in_specs.append(pl.BlockSpec((tm, tn), lambda i, j: (i, j)))
        args.append(res)
        kern = _mm_res_kernel
        out_dtype = F32
    return pl.pallas_call(
        kern,
        grid=(m // tm, n // tn),
        in_specs=in_specs,
        out_specs=pl.BlockSpec((tm, tn), lambda i, j: (i, j)),
        out_shape=jax.ShapeDtypeStruct((m, n), out_dtype),
        compiler_params=_params(("parallel", "parallel")),
        name="matmul_res" if res is not None else "matmul",
    )(*args)


def _rope_lanes(c2, cs):
    t = c2 * cs
    return t + pltpu.roll(t, MLA_ROPE, axis=1)


def _mla_inproj_kernel(x_ref, nw_ref, w_ref, qn_ref, kvn_ref, cs_ref, cq_ref, ckv_ref, kpe_ref):
    xn = _rms(x_ref[...], nw_ref[...]).astype(BF16)
    p = _dot(xn, w_ref[...])
    r = MLA_RANK
    cq_ref[...] = _rms(p[:, 0:r], qn_ref[...]).astype(BF16)
    ckv_ref[...] = _rms(p[:, r:2 * r], kvn_ref[...]).astype(BF16)
    kr = _rope_lanes(p[:, 2 * r:2 * r + LANES], cs_ref[...])
    lane = lax.broadcasted_iota(jnp.int32, kr.shape, 1)
    kpe_ref[...] = jnp.where(lane < MLA_ROPE, kr, 0.0).astype(BF16)


def mla_inproj(h, nw, w, qn, kvn, cs):
    m, k = h.shape
    n = w.shape[1]
    tm = _tile(m, 512)
    row = lambda i: (i, 0)
    const = lambda i: (0, 0)
    return pl.pallas_call(
        _mla_inproj_kernel,
        grid=(m // tm,),
        in_specs=[pl.BlockSpec((tm, k), row),
                  pl.BlockSpec((1, k), const),
                  pl.BlockSpec((k, n), const),
                  pl.BlockSpec((1, MLA_RANK), const),
                  pl.BlockSpec((1, MLA_RANK), const),
                  pl.BlockSpec((tm, LANES), row)],
        out_specs=[pl.BlockSpec((tm, MLA_RANK), row),
                   pl.BlockSpec((tm, MLA_RANK), row),
                   pl.BlockSpec((tm, LANES), row)],
        out_shape=[jax.ShapeDtypeStruct((m, MLA_RANK), BF16),
                   jax.ShapeDtypeStruct((m, MLA_RANK), BF16),
                   jax.ShapeDtypeStruct((m, LANES), BF16)],
        compiler_params=_params(("parallel",)),
        name="mla_inproj",
    )(h, nw, w, qn, kvn, cs)


def _q_up_kernel(x_ref, w_ref, cs_ref, o_ref, *, heads, scale):
    q = _dot(x_ref[...], w_ref[...])
    cs = cs_ref[...]
    hw = 2 * LANES
    for hh in range(heads):
        o_ref[:, hh * hw:hh * hw + LANES] = (q[:, hh * hw:hh * hw + LANES] * scale).astype(BF16)
        pe = _rope_lanes(q[:, hh * hw + LANES:(hh + 1) * hw], cs)
        o_ref[:, hh * hw + LANES:(hh + 1) * hw] = (pe * scale).astype(BF16)


def q_up(cq, w, cs, scale):
    m, k = cq.shape
    n = w.shape[1]
    tm = _tile(m, 512)
    tn = _tile(n, 1024)
    kern = functools.partial(_q_up_kernel, heads=tn // (2 * LANES), scale=scale)
    return pl.pallas_call(
        kern,
        grid=(m // tm, n // tn),
        in_specs=[pl.BlockSpec((tm, k), lambda i, j: (i, 0)),
                  pl.BlockSpec((k, tn), lambda i, j: (0, j)),
                  pl.BlockSpec((tm, LANES), lambda i, j: (i, 0))],
        out_specs=pl.BlockSpec((tm, tn), lambda i, j: (i, j)),
        out_shape=jax.ShapeDtypeStruct((m, n), BF16),
        compiler_params=_params(("parallel", "parallel")),
        name="q_up",
    )(cq, w, cs)


def _attn_kernel(q_ref, k_ref, kpe_ref, v_ref, o_ref, *, tq, tk):
    qi = pl.program_id(2)
    q = q_ref[...]

    def scores(kb):
        ks = pl.multiple_of(kb * tk, tk)
        kk = jnp.concatenate([k_ref[pl.ds(ks, tk), :], kpe_ref[pl.ds(ks, tk), :]], axis=1)
        return _dot_nt(q, kk), v_ref[pl.ds(ks, tk), :]

    def update(carry, s, v):
        m_prev, l_prev, acc = carry
        m_new = jnp.maximum(m_prev, jnp.max(s, axis=-1, keepdims=True))
        alpha = jnp.exp(m_prev - m_new)
        p = jnp.exp(s - m_new)
        l_new = alpha * l_prev + jnp.sum(p, axis=-1, keepdims=True)
        acc = alpha * acc + _dot(p.astype(BF16), v)
        return m_new, l_new, acc

    def body(kb, carry):
        s, v = scores(kb)
        return update(carry, s, v)

    init = (jnp.full((tq, 1), -jnp.inf, F32), jnp.zeros((tq, 1), F32),
            jnp.zeros((tq, MLA_V), F32))
    ndiag = (qi * tq) // tk
    carry = lax.fori_loop(0, ndiag, body, init)
    s, v = scores(ndiag)
    qc = lax.broadcasted_iota(jnp.int32, (tq, tk), 0) // CHUNK
    kc = lax.broadcasted_iota(jnp.int32, (tq, tk), 1) // CHUNK
    s = jnp.where(kc <= qc, s, -jnp.inf)
    _, l_fin, acc = update(carry, s, v)
    o_ref[...] = (acc / l_fin).astype(o_ref.dtype)


def attention(q, kv, kpe, seq):
    m = q.shape[0]
    nb = m // seq
    tq = _tile(seq, 256)
    tk = tq
    nq = seq // tq
    kern = functools.partial(_attn_kernel, tq=tq, tk=tk)
    return pl.pallas_call(
        kern,
        grid=(nb, MLA_HEADS, nq),
        in_specs=[pl.BlockSpec((tq, 2 * LANES), lambda b, h, i: (b * nq + i, h)),
                  pl.BlockSpec((seq, MLA_NOPE), lambda b, h, i: (b, h)),
                  pl.BlockSpec((seq, LANES), lambda b, h, i: (b, 0)),
                  pl.BlockSpec((seq, MLA_V), lambda b, h, i: (b, MLA_HEADS + h))],
        out_specs=pl.BlockSpec((tq, MLA_V), lambda b, h, i: (b * nq + i, h)),
        out_shape=jax.ShapeDtypeStruct((m, MLA_HEADS * MLA_V), BF16),
        compiler_params=_params(("parallel", "parallel", "parallel")),
        name="attention",
    )(q, kv, kpe, kv)


def _ffn_up_kernel(x_ref, nw_ref, wg_ref, wv_ref, cw_ref, cb_ref, o_ref, xn_ref, halo_ref,
                   *, tm, tiles_per_seq):
    i = pl.program_id(0)
    j = pl.program_id(1)

    @pl.when(j == 0)
    def _():
        xn_ref[...] = _rms(x_ref[...], nw_ref[...]).astype(BF16)

    @pl.when(i % tiles_per_seq == 0)
    def _():
        halo_ref[j] = jnp.zeros(halo_ref.shape[1:], F32)

    xn = xn_ref[...]
    g = _dot(xn, wg_ref[...])
    val = _dot(xn, wv_ref[...])
    gext = jnp.concatenate([halo_ref[j], g], axis=0)
    halo_ref[j] = g[tm - SUBLANES:tm, :]
    conv = cb_ref[...] + cw_ref[FFN_CONV - 1:FFN_CONV, :] * g
    for k in range(FFN_CONV - 1):
        off = SUBLANES - (FFN_CONV - 1) + k
        conv = conv + cw_ref[k:k + 1, :] * gext[off:off + tm]
    o_ref[...] = (_gelu(conv) * val).astype(o_ref.dtype)


def ffn_up(h, nw, w_up, conv_w, conv_b, seq):
    m, k = h.shape
    dff = w_up.shape[1] // 2
    tm = _tile(seq, 512)
    tn = _tile(dff, 512)
    nj = dff // tn
    kern = functools.partial(_ffn_up_kernel, tm=tm, tiles_per_seq=seq // tm)
    return pl.pallas_call(
        kern,
        grid=(m // tm, nj),
        in_specs=[pl.BlockSpec((tm, k), lambda i, j: (i, 0)),
                  pl.BlockSpec((1, k), lambda i, j: (0, 0)),
                  pl.BlockSpec((k, tn), lambda i, j: (0, j)),
                  pl.BlockSpec((k, tn), lambda i, j: (0, nj + j)),
                  pl.BlockSpec((FFN_CONV, tn), lambda i, j: (0, j)),
                  pl.BlockSpec((1, tn), lambda i, j: (0, j))],
        out_specs=pl.BlockSpec((tm, tn), lambda i, j: (i, j)),
        out_shape=jax.ShapeDtypeStruct((m, dff), BF16),
        scratch_shapes=[pltpu.VMEM((tm, k), BF16),
                        pltpu.VMEM((nj, SUBLANES, tn), F32)],
        compiler_params=_params(("arbitrary", "arbitrary")),
        name="ffn_up",
    )(h, nw, w_up, w_up, conv_w, conv_b)


def _rmsnorm_kernel(x_ref, w_ref, o_ref):
    o_ref[...] = _rms(x_ref[...], w_ref[...])


def rmsnorm(x, w):
    m, k = x.shape
    tm = _tile(m, 512)
    return pl.pallas_call(
        _rmsnorm_kernel,
        grid=(m // tm,),
        in_specs=[pl.BlockSpec((tm, k), lambda i: (i, 0)),
                  pl.BlockSpec((1, k), lambda i: (0, 0))],
        out_specs=pl.BlockSpec((tm, k), lambda i: (i, 0)),
        out_shape=jax.ShapeDtypeStruct((m, k), F32),
        compiler_params=_params(("parallel",)),
        name="final_rmsnorm",
    )(x, w)


def _rot_half_cols(w):
    half = w.shape[-1] // 2
    return jnp.concatenate([-w[..., half:], w[..., :half]], axis=-1)


def _pad_lanes(v, fill=0.0):
    n = v.shape[-1]
    pad = (-n) % LANES
    return jnp.pad(v, [(0, 0)] * (v.ndim - 1) + [(0, pad)], constant_values=fill)


def _even_layer(h, seq, nw, w_in, ln_g, ln_b, ws, bs, conv_w, conv_b, dt_bias, a_log, d_skip,
                ssm_nw, w_out):
    d = h.shape[1]
    width = d
    conv_dim = width + 2 * SSM_GROUPS * SSM_STATE
    n_main = 3 * width + conv_dim
    heads = width // SSM_HEAD_DIM
    w_main = w_in[:, :n_main].astype(BF16)
    w_dt = _pad_lanes(w_in[:, n_main:]).astype(BF16)
    proj, dt_raw = even_inproj(h, nw[None, :], w_main, w_dt)
    ya = sgu(proj, ln_g.reshape(1, width), ln_b.reshape(1, width), ws, bs.T, width)
    head_of_col = jnp.arange(width) // SSM_HEAD_DIM
    e_mat = (jnp.arange(LANES)[:, None] == head_of_col[None, :]).astype(F32)
    yb = ssd(proj, dt_raw, conv_w, conv_b[None, :], _pad_lanes(dt_bias[None, :]),
             _pad_lanes(a_log[None, :]), jnp.repeat(d_skip, SSM_HEAD_DIM)[None, :],
             ssm_nw[None, :], e_mat, seq, width)
    del heads
    y = jnp.concatenate([ya, yb], axis=1)
    return matmul(y, w_out.astype(BF16), res=h)


def _odd_layer(h, seq, cs, nw, w_in, qn, kvn, w_uq, w_ukv, w_o):
    r = MLA_RANK
    qk = MLA_NOPE + MLA_ROPE
    w_kr = w_in[:, 2 * r:]
    w_in2 = jnp.concatenate([w_in, _rot_half_cols(w_kr)], axis=1).astype(BF16)
    cq, ckv, kpe = mla_inproj(h, nw[None, :], w_in2, qn[None, :], kvn[None, :], cs)
    wq = w_uq.reshape(r, MLA_HEADS, qk)
    wq_pe = wq[:, :, MLA_NOPE:]
    wq2 = jnp.concatenate([wq[:, :, :MLA_NOPE], wq_pe, _rot_half_cols(wq_pe)], axis=-1)
    wq2 = wq2.reshape(r, MLA_HEADS * 2 * LANES).astype(BF16)
    q = q_up(cq, wq2, cs, qk ** -0.5)
    wkv = w_ukv.reshape(r, MLA_HEADS, MLA_NOPE + MLA_V)
    wkv2 = jnp.concatenate([wkv[:, :, :MLA_NOPE].reshape(r, MLA_HEADS * MLA_NOPE),
                            wkv[:, :, MLA_NOPE:].reshape(r, MLA_HEADS * MLA_V)], axis=1)
    kv = matmul(ckv, wkv2.astype(BF16), out_dtype=BF16, tn_pref=1024)
    o = attention(q, kv, kpe, seq)
    return matmul(o, w_o.astype(BF16), res=h)


def _ffn_layer(h, seq, nw, w_up, conv_w, conv_b, w_down):
    act = ffn_up(h, nw[None, :], w_up.astype(BF16), conv_w, conv_b[None, :], seq)
    return matmul(act, w_down.astype(BF16), res=h)


def kernel(x, positions, norm_mix, norm_ffn, norm_final, ev_w_in, ev_gm_ln_g, ev_gm_ln_b, ev_gm_ws, ev_gm_bs, ev_conv_w, ev_conv_b, ev_dt_bias, ev_a_log, ev_d_skip, ev_ssm_norm_w, ev_w_out, od_w_in, od_q_norm, od_kv_norm, od_w_uq, od_w_ukv, od_w_o, ff_w_up, ff_conv_w, ff_conv_b, ff_w_down):
    bsz, seq, d = x.shape
    m = bsz * seq
    depth = norm_mix.shape[0]

    inv_freq = ROPE_THETA ** (-jnp.arange(0, MLA_ROPE, 2, dtype=F32) / MLA_ROPE)
    freq_row = jnp.tile(inv_freq, 4)[None, :]
    pos_b = jnp.broadcast_to(positions.astype(F32).reshape(m, 1), (m, LANES))
    cs = rope_table(pos_b, freq_row)

    h = x.reshape(m, d)
    for layer in range(depth):
        j = layer // 2
        if layer % 2 == 0:
            h = _even_layer(h, seq, norm_mix[layer], ev_w_in[j], ev_gm_ln_g[j], ev_gm_ln_b[j],
                            ev_gm_ws[j], ev_gm_bs[j], ev_conv_w[j], ev_conv_b[j], ev_dt_bias[j],
                            ev_a_log[j], ev_d_skip[j], ev_ssm_norm_w[j], ev_w_out[j])
        else:
            h = _odd_layer(h, seq, cs, norm_mix[layer], od_w_in[j], od_q_norm[j], od_kv_norm[j],
                           od_w_uq[j], od_w_ukv[j], od_w_o[j])
        h = _ffn_layer(h, seq, norm_ffn[layer], ff_w_up[layer], ff_conv_w[layer],
                       ff_conv_b[layer], ff_w_down[layer])
    return rmsnorm(h, norm_final[None, :]).reshape(bsz, seq, d)
```

```python
import functools

import jax
import jax.numpy as jnp
from jax import lax
from jax.experimental import pallas as pl
from jax.experimental.pallas import tpu as pltpu

F32 = jnp.float32
BF16 = jnp.bfloat16

EPS = 1e-6
LANES = 128
SUBLANES = 8
HALO = 16
VMEM_LIMIT = 56 * 1024 * 1024

CHUNK = 64
GM_BLOCK = 128
GM_GROUPS = 8
SSM_HEAD_DIM = 64
SSM_GROUPS = 4
SSM_STATE = 128
SSM_CONV = 4
MLA_HEADS = 16
MLA_NOPE = 128
MLA_ROPE = 64
MLA_V = 128
MLA_RANK = 512
ATTN_HEADS_PER_STEP = 4
ROPE_THETA = 10000.0
FFN_CONV = 3


def _params(sem):
    return pltpu.CompilerParams(dimension_semantics=sem, vmem_limit_bytes=VMEM_LIMIT)


def _dot(a, b):
    return jnp.dot(a, b, preferred_element_type=F32)


def _dot_nt(a, b):
    return lax.dot_general(a, b, (((1,), (1,)), ((), ())), preferred_element_type=F32)


def _dot_tn(a, b):
    return lax.dot_general(a, b, (((0,), (0,)), ((), ())), preferred_element_type=F32)


def _rms(x, w):
    ms = jnp.mean(x * x, axis=-1, keepdims=True)
    return x * lax.rsqrt(ms + EPS) * w


def _gelu(x):
    return 0.5 * x * (1.0 + jnp.tanh(0.7978845608028654 * (x + 0.044715 * (x * x * x))))


def _silu(x):
    hx = 0.5 * x
    return hx + hx * jnp.tanh(hx)


def _split3(x):
    hi = x.astype(BF16)
    r1 = x - hi.astype(F32)
    mid = r1.astype(BF16)
    lo = (r1 - mid.astype(F32)).astype(BF16)
    return hi, mid, lo


def _expand01(x, sel):
    hi, mid, lo = _split3(x)
    return _dot(hi, sel) + _dot(mid, sel) + _dot(lo, sel)


def _tile(n, pref):
    if n <= pref:
        return n
    t = pref
    while t >= LANES:
        if n % t == 0:
            return t
        t -= LANES
    return n


def _rope_table_kernel(pos_ref, freq_ref, o_ref):
    ang = pos_ref[...] * freq_ref[...]
    lane = lax.broadcasted_iota(jnp.int32, ang.shape, 1)
    o_ref[...] = jnp.where(lane < MLA_ROPE, jnp.cos(ang), jnp.sin(ang))


def rope_table(pos_b, freq_row):
    m = pos_b.shape[0]
    tm = _tile(m, 1024)
    return pl.pallas_call(
        _rope_table_kernel,
        grid=(m // tm,),
        in_specs=[pl.BlockSpec((tm, LANES), lambda i: (i, 0)),
                  pl.BlockSpec((1, LANES), lambda i: (0, 0))],
        out_specs=pl.BlockSpec((tm, LANES), lambda i: (i, 0)),
        out_shape=jax.ShapeDtypeStruct((m, LANES), F32),
        compiler_params=_params(("parallel",)),
        name="rope_table",
    )(pos_b, freq_row)


def _even_inproj_kernel(x_ref, nw_ref, w_ref, wdt_ref, o_ref, dt_ref, xn_ref):
    @pl.when(pl.program_id(1) == 0)
    def _():
        xn = _rms(x_ref[...], nw_ref[...]).astype(BF16)
        xn_ref[...] = xn
        dt_ref[...] = _dot(xn, wdt_ref[...])

    o_ref[...] = _dot(xn_ref[...], w_ref[...].astype(BF16)).astype(o_ref.dtype)


def even_inproj(h, nw, w_all, layer, n_main, wdt):
    m, k = h.shape
    tm = _tile(m, 1024)
    tn = _tile(n_main, 512)
    return pl.pallas_call(
        _even_inproj_kernel,
        grid=(m // tm, n_main // tn),
        in_specs=[pl.BlockSpec((tm, k), lambda i, j: (i, 0)),
                  pl.BlockSpec((1, k), lambda i, j: (0, 0)),
                  pl.BlockSpec((None, k, tn), lambda i, j: (layer, 0, j)),
                  pl.BlockSpec((k, LANES), lambda i, j: (0, 0))],
        out_specs=[pl.BlockSpec((tm, tn), lambda i, j: (i, j)),
                   pl.BlockSpec((tm, LANES), lambda i, j: (i, 0))],
        out_shape=[jax.ShapeDtypeStruct((m, n_main), BF16),
                   jax.ShapeDtypeStruct((m, LANES), F32)],
        scratch_shapes=[pltpu.VMEM((tm, k), BF16)],
        compiler_params=_params(("parallel", "arbitrary")),
        name="even_inproj",
    )(h, nw, w_all, wdt)


def _sgu_kernel(u_ref, v_ref, g_ref, b_ref, ws_ref, bs_ref, o_ref, *, nblk, gd):
    row = lax.broadcasted_iota(jnp.int32, (GM_BLOCK, GM_BLOCK), 0) // CHUNK
    col = lax.broadcasted_iota(jnp.int32, (GM_BLOCK, GM_BLOCK), 1) // CHUNK
    keep = row >= col
    for g in range(GM_GROUPS):
        cols = slice(g * gd, (g + 1) * gd)
        wsg = jnp.where(keep, ws_ref[g], 0.0).astype(BF16)
        ln_g = g_ref[:, cols]
        ln_b = b_ref[:, cols]
        bsg = bs_ref[:, g:g + 1]
        for nb in range(nblk):
            rows = slice(nb * GM_BLOCK, (nb + 1) * GM_BLOCK)
            v = _gelu(v_ref[rows, cols].astype(F32))
            mu = jnp.mean(v, axis=-1, keepdims=True)
            d = v - mu
            var = jnp.mean(d * d, axis=-1, keepdims=True)
            vn = (d * lax.rsqrt(var + EPS) * ln_g + ln_b).astype(BF16)
            gate = _dot(wsg, vn) + bsg
            o_ref[rows, cols] = (_gelu(u_ref[rows, cols].astype(F32)) * gate).astype(o_ref.dtype)


def sgu(proj, ln_g, ln_b, ws, bs_t, width):
    m = proj.shape[0]
    gd = width // GM_GROUPS
    nblk = 2 if m % (2 * GM_BLOCK) == 0 else 1
    t = nblk * GM_BLOCK
    return pl.pallas_call(
        functools.partial(_sgu_kernel, nblk=nblk, gd=gd),
        grid=(m // t,),
        in_specs=[pl.BlockSpec((t, width), lambda i: (i, 0)),
                  pl.BlockSpec((t, width), lambda i: (i, 1)),
                  pl.BlockSpec((1, width), lambda i: (0, 0)),
                  pl.BlockSpec((1, width), lambda i: (0, 0)),
                  pl.BlockSpec((GM_GROUPS, GM_BLOCK, GM_BLOCK), lambda i: (0, 0, 0)),
                  pl.BlockSpec((GM_BLOCK, GM_GROUPS), lambda i: (0, 0))],
        out_specs=pl.BlockSpec((t, width), lambda i: (i, 0)),
        out_shape=jax.ShapeDtypeStruct((m, width), BF16),
        compiler_params=_params(("parallel",)),
        name="sgu",
    )(proj, proj, ln_g, ln_b, ws, bs_t)


def _ssd_kernel(z_ref, xbc_ref, dt_ref, cw_ref, cb_ref, dtb_ref, alog_ref, dsk_ref, nw_ref,
                e_ref, o_ref, xe_ref, xc_ref, xd_ref, da_ref, y_ref, h_ref,
                *, t_rows, d_inner):
    L = CHUNK
    P = SSM_HEAD_DIM
    N = SSM_STATE
    G = SSM_GROUPS
    gw = d_inner // G
    pairs_per_group = gw // LANES
    conv_dim = d_inner + 2 * G * N
    cchunk = 512

    @pl.when(pl.program_id(1) == 0)
    def _():
        xe_ref[0:HALO, :] = jnp.zeros((HALO, conv_dim), F32)
        h_ref[...] = jnp.zeros_like(h_ref)

    @pl.when(pl.program_id(1) != 0)
    def _():
        xe_ref[0:HALO, :] = xe_ref[t_rows:t_rows + HALO, :]

    for c0 in range(0, conv_dim, cchunk):
        cs_ = slice(c0, c0 + cchunk)
        xe_ref[HALO:HALO + t_rows, cs_] = xbc_ref[:, cs_].astype(F32)
        acc = cb_ref[:, cs_]
        for k in range(SSM_CONV):
            off = HALO - (SSM_CONV - 1) + k
            acc = acc + cw_ref[k:k + 1, cs_] * xe_ref[off:off + t_rows, cs_]
        xc_ref[:, cs_] = _silu(acc)

    dtr = dt_ref[...] + dtb_ref[...]
    dt = jnp.maximum(dtr, 0.0) + jnp.log1p(jnp.exp(-jnp.abs(dtr)))
    a = -jnp.exp(alog_ref[...])
    da_ref[...] = dt * a
    e_mat = e_ref[...]
    xd_ref[...] = xc_ref[:, 0:d_inner] * _expand01(dt, e_mat)

    ri = lax.broadcasted_iota(jnp.int32, (L, L), 0)
    ci = lax.broadcasted_iota(jnp.int32, (L, L), 1)
    tril = jnp.where(ri >= ci, 1.0, 0.0).astype(BF16)
    ri2 = lax.broadcasted_iota(jnp.int32, (L, LANES), 0)
    ci2 = lax.broadcasted_iota(jnp.int32, (L, LANES), 1)
    left = ci2 < P
    lower2 = ri2 >= jnp.where(left, ci2, ci2 - P)

    def chunk(c, carry):
        r0 = pl.multiple_of(c * L, L)
        rows = pl.ds(r0, L)
        d_hi, d_mid, d_lo = _split3(da_ref[rows, :])
        cum = _dot(tril, d_hi) + _dot(tril, d_mid) + _dot(tril, d_lo)
        cum_all = _expand01(cum, e_mat)
        cum_t = cum.T
        for g in range(G):
            gc = slice(g * gw, (g + 1) * gw)
            b_g = xc_ref[rows, d_inner + g * N:d_inner + (g + 1) * N].astype(BF16)
            c_g = xc_ref[rows, d_inner + G * N + g * N:d_inner + G * N + (g + 1) * N].astype(BF16)
            cb = _dot_nt(c_g, b_g)
            cb2 = jnp.concatenate([cb, cb], axis=1)
            xd_g = xd_ref[rows, gc]
            cum_x = cum_all[:, gc]
            eac_x = jnp.exp(cum_x)
            dst_x = jnp.exp(cum_x[L - 1:L, :] - cum_x)
            h_g = h_ref[g]
            y_off = _dot(c_g, h_g.astype(BF16)) * eac_x
            st = _dot_tn(b_g, (xd_g * dst_x).astype(BF16))
            h_ref[g] = h_g * eac_x[L - 1:L, :] + st
            parts = []
            for q in range(pairs_per_group):
                hd = (g * pairs_per_group + q) * 2
                lanes = slice(q * LANES, (q + 1) * LANES)
                col_b = cum_x[:, lanes]
                row_b = jnp.concatenate([cum_t[hd:hd + 1, :], cum_t[hd + 1:hd + 2, :]], axis=1)
                dec = jnp.exp(jnp.where(lower2, col_b - row_b, -jnp.inf))
                mm = (cb2 * dec).astype(BF16)
                slab = xd_g[:, lanes]
                rhs = jnp.concatenate([jnp.where(left, slab, 0.0), jnp.where(left, 0.0, slab)],
                                      axis=0).astype(BF16)
                parts.append(_dot(mm, rhs))
            y_ref[rows, gc] = jnp.concatenate(parts, axis=1) + y_off
        return carry

    lax.fori_loop(0, t_rows // L, chunk, 0)

    for g in range(G):
        gc = slice(g * gw, (g + 1) * gw)
        y = y_ref[:, gc] + dsk_ref[:, gc] * xc_ref[:, gc]
        y = y * _silu(z_ref[:, gc].astype(F32))
        o_ref[:, gc] = _rms(y, nw_ref[:, gc]).astype(o_ref.dtype)


def ssd(proj, dt_raw, conv_w, conv_b, dt_bias, a_log, d_skip_x, norm_w, e_mat, seq, d_inner):
    m = proj.shape[0]
    conv_dim = d_inner + 2 * SSM_GROUPS * SSM_STATE
    assert 2 * SSM_HEAD_DIM == LANES
    t = _tile(seq, 256)
    spb = seq // t
    nb = m // seq
    z_blk = 2
    xbc_blk = (3 * d_inner) // conv_dim
    assert 3 * d_inner == xbc_blk * conv_dim
    kern = functools.partial(_ssd_kernel, t_rows=t, d_inner=d_inner)
    row = lambda b, s: (b * spb + s, 0)
    const = lambda b, s: (0, 0)
    return pl.pallas_call(
        kern,
        grid=(nb, spb),
        in_specs=[pl.BlockSpec((t, d_inner), lambda b, s: (b * spb + s, z_blk)),
                  pl.BlockSpec((t, conv_dim), lambda b, s: (b * spb + s, xbc_blk)),
                  pl.BlockSpec((t, LANES), row),
                  pl.BlockSpec((SSM_CONV, conv_dim), const),
                  pl.BlockSpec((1, conv_dim), const),
                  pl.BlockSpec((1, LANES), const),
                  pl.BlockSpec((1, LANES), const),
                  pl.BlockSpec((1, d_inner), const),
                  pl.BlockSpec((1, d_inner), const),
                  pl.BlockSpec((LANES, d_inner), const)],
        out_specs=pl.BlockSpec((t, d_inner), row),
        out_shape=jax.ShapeDtypeStruct((m, d_inner), BF16),
        scratch_shapes=[pltpu.VMEM((HALO + t, conv_dim), F32),
                        pltpu.VMEM((t, conv_dim), F32),
                        pltpu.VMEM((t, d_inner), F32),
                        pltpu.VMEM((t, LANES), F32),
                        pltpu.VMEM((t, d_inner), F32),
                        pltpu.VMEM((SSM_GROUPS, SSM_STATE, d_inner // SSM_GROUPS), F32)],
        compiler_params=_params(("arbitrary", "arbitrary")),
        name="ssd",
    )(proj, proj, dt_raw, conv_w, conv_b, dt_bias, a_log, d_skip_x, norm_w, e_mat)


def _mm_kernel(x_ref, w_ref, o_ref):
    o_ref[...] = _dot(x_ref[...], w_ref[...].astype(BF16)).astype(o_ref.dtype)


def matmul(x, w, out_dtype, tm_pref=1024, tn_pref=1024):
    m, k = x.shape
    n = w.shape[1]
    tm = _tile(m, tm_pref)
    tn = _tile(n, tn_pref)
    return pl.pallas_call(
        _mm_kernel,
        grid=(m // tm, n // tn),
        in_specs=[pl.BlockSpec((tm, k), lambda i, j: (i, 0)),
                  pl.BlockSpec((k, tn), lambda i, j: (0, j))],
        out_specs=pl.BlockSpec((tm, tn), lambda i, j: (i, j)),
        out_shape=jax.ShapeDtypeStruct((m, n), out_dtype),
        compiler_params=_params(("parallel", "parallel")),
        name="matmul",
    )(x, w)


def _mm_res_kernel(x_ref, w_ref, r_ref, o_ref):
    o_ref[...] = r_ref[...] + _dot(x_ref[...], w_ref[...].astype(BF16))


def matmul_res(x, w_all, layer, res, tn_pref):
    m, k = x.shape
    n = w_all.shape[-1]
    tm = _tile(m, 1024)
    tn = _tile(n, tn_pref)
    return pl.pallas_call(
        _mm_res_kernel,
        grid=(m // tm, n // tn),
        in_specs=[pl.BlockSpec((tm, k), lambda i, j: (i, 0)),
                  pl.BlockSpec((None, k, tn), lambda i, j: (layer, 0, j)),
                  pl.BlockSpec((tm, tn), lambda i, j: (i, j))],
        out_specs=pl.BlockSpec((tm, tn), lambda i, j: (i, j)),
        out_shape=jax.ShapeDtypeStruct((m, n), F32),
        compiler_params=_params(("parallel", "parallel")),
        name="matmul_res",
    )(x, w_all, res)


def _mm2_res_kernel(xa_ref, xb_ref, wa_ref, wb_ref, r_ref, o_ref):
    o_ref[...] = (r_ref[...] + _dot(xa_ref[...], wa_ref[...].astype(BF16))
                  + _dot(xb_ref[...], wb_ref[...].astype(BF16)))


def matmul2_res(xa, xb, w_all, layer, res, tn_pref):
    m, k = xa.shape
    n = w_all.shape[-1]
    tm = _tile(m, 1024)
    tn = _tile(n, tn_pref)
    return pl.pallas_call(
        _mm2_res_kernel,
        grid=(m // tm, n // tn),
        in_specs=[pl.BlockSpec((tm, k), lambda i, j: (i, 0)),
                  pl.BlockSpec((tm, k), lambda i, j: (i, 0)),
                  pl.BlockSpec((None, k, tn), lambda i, j: (layer, 0, j)),
                  pl.BlockSpec((None, k, tn), lambda i, j: (layer, 1, j)),
                  pl.BlockSpec((tm, tn), lambda i, j: (i, j))],
        out_specs=pl.BlockSpec((tm, tn), lambda i, j: (i, j)),
        out_shape=jax.ShapeDtypeStruct((m, n), F32),
        compiler_params=_params(("parallel", "parallel")),
        name="matmul2_res",
    )(xa, xb, w_all, w_all, res)


def _rope_lanes(c2, cs):
    t = c2 * cs
    return t + pltpu.roll(t, MLA_ROPE, axis=1)


def _mla_inproj_kernel(x_ref, nw_ref, w_ref, qn_ref, kvn_ref, cs_ref, cq_ref, ckv_ref, kpe_ref):
    xn = _rms(x_ref[...], nw_ref[...]).astype(BF16)
    p = _dot(xn, w_ref[...])
    r = MLA_RANK
    cq_ref[...] = _rms(p[:, 0:r], qn_ref[...]).astype(BF16)
    ckv_ref[...] = _rms(p[:, r:2 * r], kvn_ref[...]).astype(BF16)
    kr = _rope_lanes(p[:, 2 * r:2 * r + LANES], cs_ref[...])
    lane = lax.broadcasted_iota(jnp.int32, kr.shape, 1)
    kpe_ref[...] = jnp.where(lane < MLA_ROPE, kr, 0.0).astype(BF16)


def mla_inproj(h, nw, w, qn, kvn, cs):
    m, k = h.shape
    n = w.shape[1]
    tm = _tile(m, 512)
    row = lambda i: (i, 0)
    const = lambda i: (0, 0)
    return pl.pallas_call(
        _mla_inproj_kernel,
        grid=(m // tm,),
        in_specs=[pl.BlockSpec((tm, k), row),
                  pl.BlockSpec((1, k), const),
                  pl.BlockSpec((k, n), const),
                  pl.BlockSpec((1, MLA_RANK), const),
                  pl.BlockSpec((1, MLA_RANK), const),
                  pl.BlockSpec((tm, LANES), row)],
        out_specs=[pl.BlockSpec((tm, MLA_RANK), row),
                   pl.BlockSpec((tm, MLA_RANK), row),
                   pl.BlockSpec((tm, LANES), row)],
        out_shape=[jax.ShapeDtypeStruct((m, MLA_RANK), BF16),
                   jax.ShapeDtypeStruct((m, MLA_RANK), BF16),
                   jax.ShapeDtypeStruct((m, LANES), BF16)],
        compiler_params=_params(("parallel",)),
        name="mla_inproj",
    )(h, nw, w, qn, kvn, cs)


def _q_up_kernel(x_ref, w_ref, cs_ref, o_ref, *, heads, scale):
    q = _dot(x_ref[...], w_ref[...])
    cs = cs_ref[...]
    hw = 2 * LANES
    for hh in range(heads):
        o_ref[:, hh * hw:hh * hw + LANES] = (q[:, hh * hw:hh * hw + LANES] * scale).astype(BF16)
        pe = _rope_lanes(q[:, hh * hw + LANES:(hh + 1) * hw], cs)
        o_ref[:, hh * hw + LANES:(hh + 1) * hw] = (pe * scale).astype(BF16)


def q_up(cq, w, cs, scale):
    m, k = cq.shape
    n = w.shape[1]
    tm = _tile(m, 512)
    tn = _tile(n, 1024)
    kern = functools.partial(_q_up_kernel, heads=tn // (2 * LANES), scale=scale)
    return pl.pallas_call(
        kern,
        grid=(m // tm, n // tn),
        in_specs=[pl.BlockSpec((tm, k), lambda i, j: (i, 0)),
                  pl.BlockSpec((k, tn), lambda i, j: (0, j)),
                  pl.BlockSpec((tm, LANES), lambda i, j: (i, 0))],
        out_specs=pl.BlockSpec((tm, tn), lambda i, j: (i, j)),
        out_shape=jax.ShapeDtypeStruct((m, n), BF16),
        compiler_params=_params(("parallel", "parallel")),
        name="q_up",
    )(cq, w, cs)


def _attn_kernel(q_ref, k_ref, kpe_ref, v_ref, o_ref, vt_ref, acc_ref, *, tq, tk, seq, hpg):
    i = pl.program_id(2)
    qw = 2 * LANES

    @pl.when(i == 0)
    def _():
        for h in range(hpg):
            for c in range(seq // tk):
                blk = v_ref[c * tk:(c + 1) * tk, h * MLA_V:(h + 1) * MLA_V].astype(F32)
                vt_ref[h, c] = blk.T.astype(BF16)

    acc_ref[...] = jnp.zeros(acc_ref.shape, F32)

    def block(kb, carry, masked):
        ms, ls = carry
        ks = pl.multiple_of(kb * tk, tk)
        kpe = kpe_ref[pl.ds(ks, tk), :]
        ss = []
        for h in range(hpg):
            kk = jnp.concatenate([k_ref[pl.ds(ks, tk), h * MLA_NOPE:(h + 1) * MLA_NOPE], kpe],
                                 axis=1)
            s = _dot_nt(kk, q_ref[:, h * qw:(h + 1) * qw])
            if masked:
                kc = lax.broadcasted_iota(jnp.int32, (tk, tq), 0) // CHUNK
                qc = lax.broadcasted_iota(jnp.int32, (tk, tq), 1) // CHUNK
                s = jnp.where(kc <= qc, s, -jnp.inf)
            ss.append(s)
        new_ms, new_ls, alphas, ps = [], [], [], []
        for h in range(hpg):
            m_new = jnp.maximum(ms[h], jnp.max(ss[h], axis=0, keepdims=True))
            alpha = jnp.exp(ms[h] - m_new)
            p = jnp.exp(ss[h] - m_new)
            new_ls.append(alpha * ls[h] + jnp.sum(p, axis=0, keepdims=True))
            new_ms.append(m_new)
            alphas.append(alpha)
            ps.append(p.astype(BF16))
        pvs = [_dot(vt_ref[h, kb], ps[h]) for h in range(hpg)]
        for h in range(hpg):
            acc_ref[h] = alphas[h] * acc_ref[h] + pvs[h]
        return tuple(new_ms), tuple(new_ls)

    init = (tuple(jnp.full((1, tq), -jnp.inf, F32) for _ in range(hpg)),
            tuple(jnp.zeros((1, tq), F32) for _ in range(hpg)))
    ndiag = (i * tq) // tk
    carry = lax.fori_loop(0, ndiag, lambda kb, c: block(kb, c, False), init)
    _, ls = block(ndiag, carry, True)

    for h in range(hpg):
        o = acc_ref[h] * (1.0 / ls[h])
        o_ref[:, h * MLA_V:(h + 1) * MLA_V] = o.T.astype(o_ref.dtype)


def attention(q, kv, kpe, seq):
    m = q.shape[0]
    nb = m // seq
    tq = _tile(seq, 256)
    tk = tq
    nq = seq // tq
    hpg = ATTN_HEADS_PER_STEP
    ng = MLA_HEADS // hpg
    kern = functools.partial(_attn_kernel, tq=tq, tk=tk, seq=seq, hpg=hpg)
    return pl.pallas_call(
        kern,
        grid=(nb, ng, nq),
        in_specs=[pl.BlockSpec((tq, hpg * 2 * LANES), lambda b, g, i: (b * nq + i, g)),
                  pl.BlockSpec((seq, hpg * MLA_NOPE), lambda b, g, i: (b, g)),
                  pl.BlockSpec((seq, LANES), lambda b, g, i: (b, 0)),
                  pl.BlockSpec((seq, hpg * MLA_V), lambda b, g, i: (b, ng + g))],
        out_specs=pl.BlockSpec((tq, hpg * MLA_V), lambda b, g, i: (b * nq + i, g)),
        out_shape=jax.ShapeDtypeStruct((m, MLA_HEADS * MLA_V), BF16),
        scratch_shapes=[pltpu.VMEM((hpg, seq // tk, MLA_V, tk), BF16),
                        pltpu.VMEM((hpg, MLA_V, tq), F32)],
        compiler_params=_params(("parallel", "parallel", "arbitrary")),
        name="attention",
    )(q, kv, kpe, kv)


def _ffn_up_kernel(x_ref, nw_ref, wg_ref, wv_ref, cw_ref, cb_ref, o_ref, xn_ref, halo_ref,
                   *, tm, tiles_per_seq):
    i = pl.program_id(0)
    j = pl.program_id(1)

    @pl.when(j == 0)
    def _():
        xn_ref[...] = _rms(x_ref[...], nw_ref[...]).astype(BF16)

    @pl.when(i % tiles_per_seq == 0)
    def _():
        halo_ref[j] = jnp.zeros(halo_ref.shape[1:], F32)

    xn = xn_ref[...]
    g = _dot(xn, wg_ref[...].astype(BF16))
    val = _dot(xn, wv_ref[...].astype(BF16))
    gext = jnp.concatenate([halo_ref[j], g], axis=0)
    halo_ref[j] = g[tm - SUBLANES:tm, :]
    conv = cb_ref[...] + cw_ref[FFN_CONV - 1:FFN_CONV, :] * g
    for k in range(FFN_CONV - 1):
        off = SUBLANES - (FFN_CONV - 1) + k
        conv = conv + cw_ref[k:k + 1, :] * gext[off:off + tm]
    o_ref[...] = (_gelu(conv) * val).astype(o_ref.dtype)


def ffn_up(h, nw, w_up_all, conv_w_all, conv_b_all, layer, seq):
    m, k = h.shape
    dff = w_up_all.shape[-1] // 2
    tm = _tile(seq, 1024)
    tn = _tile(dff, 512)
    nj = dff // tn
    kern = functools.partial(_ffn_up_kernel, tm=tm, tiles_per_seq=seq // tm)
    return pl.pallas_call(
        kern,
        grid=(m // tm, nj),
        in_specs=[pl.BlockSpec((tm, k), lambda i, j: (i, 0), pipeline_mode=pl.Buffered(1)),
                  pl.BlockSpec((1, k), lambda i, j: (0, 0)),
                  pl.BlockSpec((None, k, tn), lambda i, j: (layer, 0, j)),
                  pl.BlockSpec((None, k, tn), lambda i, j: (layer, 0, nj + j)),
                  pl.BlockSpec((None, FFN_CONV, tn), lambda i, j: (layer, 0, j)),
                  pl.BlockSpec((None, 1, tn), lambda i, j: (layer, 0, j))],
        out_specs=pl.BlockSpec((tm, tn), lambda i, j: (i, j)),
        out_shape=jax.ShapeDtypeStruct((m, dff), BF16),
        scratch_shapes=[pltpu.VMEM((tm, k), BF16),
                        pltpu.VMEM((nj, SUBLANES, tn), F32)],
        compiler_params=_params(("arbitrary", "arbitrary")),
        name="ffn_up",
    )(h, nw, w_up_all, w_up_all, conv_w_all, conv_b_all)


def _rmsnorm_kernel(x_ref, w_ref, o_ref):
    o_ref[...] = _rms(x_ref[...], w_ref[...])


def rmsnorm(x, w):
    m, k = x.shape
    tm = _tile(m, 512)
    return pl.pallas_call(
        _rmsnorm_kernel,
        grid=(m // tm,),
        in_specs=[pl.BlockSpec((tm, k), lambda i: (i, 0)),
                  pl.BlockSpec((1, k), lambda i: (0, 0))],
        out_specs=pl.BlockSpec((tm, k), lambda i: (i, 0)),
        out_shape=jax.ShapeDtypeStruct((m, k), F32),
        compiler_params=_params(("parallel",)),
        name="final_rmsnorm",
    )(x, w)


def _rot_half_cols(w):
    half = w.shape[-1] // 2
    return jnp.concatenate([-w[..., half:], w[..., :half]], axis=-1)


def _pad_lanes(v):
    pad = (-v.shape[-1]) % LANES
    return jnp.pad(v, [(0, 0)] * (v.ndim - 1) + [(0, pad)])


def _even_layer(h, seq, nw, j, ev_w_in, ln_g, ln_b, ws, bs, conv_w, conv_b, dt_bias, a_log,
                d_skip, ssm_nw, ev_w_out):
    width = h.shape[1]
    conv_dim = width + 2 * SSM_GROUPS * SSM_STATE
    n_main = 3 * width + conv_dim
    w_dt = _pad_lanes(ev_w_in[j, :, n_main:]).astype(BF16)
    proj, dt_raw = even_inproj(h, nw[None, :], ev_w_in, j, n_main, w_dt)
    ya = sgu(proj, ln_g.reshape(1, width), ln_b.reshape(1, width), ws, bs.T, width)
    head_of_col = jnp.arange(width) // SSM_HEAD_DIM
    e_mat = (jnp.arange(LANES)[:, None] == head_of_col[None, :]).astype(BF16)
    yb = ssd(proj, dt_raw, conv_w, conv_b[None, :], _pad_lanes(dt_bias[None, :]),
             _pad_lanes(a_log[None, :]), jnp.repeat(d_skip, SSM_HEAD_DIM)[None, :],
             ssm_nw[None, :], e_mat, seq, width)
    return matmul2_res(ya, yb, ev_w_out, j, h, 512)


def _odd_layer(h, seq, cs, nw, j, w_in, qn, kvn, w_uq, w_ukv, od_w_o):
    r = MLA_RANK
    qk = MLA_NOPE + MLA_ROPE
    w_kr = w_in[:, 2 * r:]
    w_in2 = jnp.concatenate([w_in, _rot_half_cols(w_kr)], axis=1).astype(BF16)
    cq, ckv, kpe = mla_inproj(h, nw[None, :], w_in2, qn[None, :], kvn[None, :], cs)
    wq = w_uq.reshape(r, MLA_HEADS, qk)
    wq_pe = wq[:, :, MLA_NOPE:]
    wq2 = jnp.concatenate([wq[:, :, :MLA_NOPE], wq_pe, _rot_half_cols(wq_pe)], axis=-1)
    wq2 = wq2.reshape(r, MLA_HEADS * 2 * LANES).astype(BF16)
    q = q_up(cq, wq2, cs, qk ** -0.5)
    wkv = w_ukv.reshape(r, MLA_HEADS, MLA_NOPE + MLA_V)
    wkv2 = jnp.concatenate([wkv[:, :, :MLA_NOPE].reshape(r, MLA_HEADS * MLA_NOPE),
                            wkv[:, :, MLA_NOPE:].reshape(r, MLA_HEADS * MLA_V)], axis=1)
    kv = matmul(ckv, wkv2.astype(BF16), BF16)
    o = attention(q, kv, kpe, seq)
    return matmul_res(o, od_w_o, j, h, 512)


def kernel(x, positions, norm_mix, norm_ffn, norm_final, ev_w_in, ev_gm_ln_g, ev_gm_ln_b, ev_gm_ws, ev_gm_bs, ev_conv_w, ev_conv_b, ev_dt_bias, ev_a_log, ev_d_skip, ev_ssm_norm_w, ev_w_out, od_w_in, od_q_norm, od_kv_norm, od_w_uq, od_w_ukv, od_w_o, ff_w_up, ff_conv_w, ff_conv_b, ff_w_down):
    bsz, seq, d = x.shape
    m = bsz * seq
    depth = norm_mix.shape[0]

    inv_freq = ROPE_THETA ** (-jnp.arange(0, MLA_ROPE, 2, dtype=F32) / MLA_ROPE)
    freq_row = jnp.tile(inv_freq, 4)[None, :]
    pos_b = jnp.broadcast_to(positions.astype(F32).reshape(m, 1), (m, LANES))
    cs = rope_table(pos_b, freq_row)
    ff_conv_b3 = ff_conv_b[:, None, :]

    h = x.reshape(m, d)
    for layer in range(depth):
        j = layer // 2
        if layer % 2 == 0:
            h = _even_layer(h, seq, norm_mix[layer], j, ev_w_in, ev_gm_ln_g[j], ev_gm_ln_b[j],
                            ev_gm_ws[j], ev_gm_bs[j], ev_conv_w[j], ev_conv_b[j], ev_dt_bias[j],
                            ev_a_log[j], ev_d_skip[j], ev_ssm_norm_w[j], ev_w_out)
        else:
            h = _odd_layer(h, seq, cs, norm_mix[layer], j, od_w_in[j], od_q_norm[j],
                           od_kv_norm[j], od_w_uq[j], od_w_ukv[j], od_w_o)
        act = ffn_up(h, norm_ffn[layer][None, :], ff_w_up, ff_conv_w, ff_conv_b3, layer, seq)
        h = matmul_res(act, ff_w_down, layer, h, 256)
    return rmsnorm(h, norm_final[None, :]).reshape(bsz, seq, d)
```

```python
import functools

import jax
import jax.numpy as jnp
from jax import lax
from jax.experimental import pallas as pl
from jax.experimental.pallas import tpu as pltpu

F32 = jnp.float32
BF16 = jnp.bfloat16

EPS = 1e-6
LANES = 128
SUBLANES = 8
HALO = 16
VMEM_LIMIT = 56 * 1024 * 1024

CHUNK = 64
GM_BLOCK = 128
GM_GROUPS = 8
SSM_HEAD_DIM = 64
SSM_GROUPS = 4
SSM_STATE = 128
SSM_CONV = 4
MLA_HEADS = 16
MLA_NOPE = 128
MLA_ROPE = 64
MLA_V = 128
MLA_RANK = 512
ATTN_HEADS_PER_STEP = 8
ROPE_THETA = 10000.0
LOG2E = 1.4426950408889634
FFN_CONV = 3


def _params(sem):
    return pltpu.CompilerParams(dimension_semantics=sem, vmem_limit_bytes=VMEM_LIMIT)


def _dot(a, b):
    return jnp.dot(a, b, preferred_element_type=F32)


def _dot_nt(a, b):
    return lax.dot_general(a, b, (((1,), (1,)), ((), ())), preferred_element_type=F32)


def _dot_tn(a, b):
    return lax.dot_general(a, b, (((0,), (0,)), ((), ())), preferred_element_type=F32)


def _rms(x, w):
    ms = jnp.mean(x * x, axis=-1, keepdims=True)
    return x * lax.rsqrt(ms + EPS) * w


def _gelu(x):
    return 0.5 * x * (1.0 + jnp.tanh(0.7978845608028654 * (x + 0.044715 * (x * x * x))))


def _silu(x):
    hx = 0.5 * x
    return hx + hx * jnp.tanh(hx)


def _split3(x):
    hi = x.astype(BF16)
    r1 = x - hi.astype(F32)
    mid = r1.astype(BF16)
    lo = (r1 - mid.astype(F32)).astype(BF16)
    return hi, mid, lo


def _expand01(x, sel):
    hi, mid, lo = _split3(x)
    return _dot(hi, sel) + _dot(mid, sel) + _dot(lo, sel)


def _tile(n, pref):
    if n <= pref:
        return n
    t = pref
    while t >= LANES:
        if n % t == 0:
            return t
        t -= LANES
    return n


def _rope_table_kernel(pos_ref, freq_ref, o_ref):
    ang = pos_ref[...] * freq_ref[...]
    lane = lax.broadcasted_iota(jnp.int32, ang.shape, 1)
    o_ref[...] = jnp.where(lane < MLA_ROPE, jnp.cos(ang), jnp.sin(ang))


def rope_table(pos_b, freq_row):
    m = pos_b.shape[0]
    tm = _tile(m, 1024)
    return pl.pallas_call(
        _rope_table_kernel,
        grid=(m // tm,),
        in_specs=[pl.BlockSpec((tm, LANES), lambda i: (i, 0)),
                  pl.BlockSpec((1, LANES), lambda i: (0, 0))],
        out_specs=pl.BlockSpec((tm, LANES), lambda i: (i, 0)),
        out_shape=jax.ShapeDtypeStruct((m, LANES), F32),
        compiler_params=_params(("parallel",)),
        name="rope_table",
    )(pos_b, freq_row)


def _transpose_cast_kernel(x_ref, o_ref):
    o_ref[...] = x_ref[...].T.astype(o_ref.dtype)


def transpose_cast(wt_all, n_main):
    nl, _, k = wt_all.shape
    tn = _tile(n_main, 512)
    return pl.pallas_call(
        _transpose_cast_kernel,
        grid=(nl, n_main // tn),
        in_specs=[pl.BlockSpec((None, tn, k), lambda l, j: (l, j, 0))],
        out_specs=pl.BlockSpec((None, k, tn), lambda l, j: (l, 0, j)),
        out_shape=jax.ShapeDtypeStruct((nl, k, n_main), BF16),
        compiler_params=_params(("parallel", "parallel")),
        name="transpose_cast",
    )(wt_all)


def _even_inproj_kernel(x_ref, nw_ref, w_ref, wdt_ref, o_ref, dt_ref, xn_ref):
    @pl.when(pl.program_id(1) == 0)
    def _():
        xn = _rms(x_ref[...], nw_ref[...]).astype(BF16)
        xn_ref[...] = xn
        n_dt, k = wdt_ref.shape
        wdt = jnp.concatenate([wdt_ref[...], jnp.zeros((LANES - n_dt, k), F32)], axis=0)
        dt_ref[...] = _dot_nt(xn, wdt.astype(BF16))

    o_ref[...] = _dot(xn_ref[...], w_ref[...]).astype(o_ref.dtype)


def even_inproj(h, nw, w_main, wt_all, layer):
    m, k = h.shape
    n_main = w_main.shape[-1]
    n_dt = wt_all.shape[1] - n_main
    assert 0 < n_dt < LANES and n_dt % SUBLANES == 0 and n_main % n_dt == 0
    tm = _tile(m, 1024)
    tn = _tile(n_main, 1024)
    return pl.pallas_call(
        _even_inproj_kernel,
        grid=(m // tm, n_main // tn),
        in_specs=[pl.BlockSpec((tm, k), lambda i, j: (i, 0)),
                  pl.BlockSpec((1, k), lambda i, j: (0, 0)),
                  pl.BlockSpec((None, k, tn), lambda i, j: (layer, 0, j)),
                  pl.BlockSpec((None, n_dt, k), lambda i, j: (layer, n_main // n_dt, 0))],
        out_specs=[pl.BlockSpec((tm, tn), lambda i, j: (i, j)),
                   pl.BlockSpec((tm, LANES), lambda i, j: (i, 0))],
        out_shape=[jax.ShapeDtypeStruct((m, n_main), BF16),
                   jax.ShapeDtypeStruct((m, LANES), F32)],
        scratch_shapes=[pltpu.VMEM((tm, k), BF16)],
        compiler_params=_params(("parallel", "arbitrary")),
        name="even_inproj",
    )(h, nw, w_main, wt_all)


def _sgu_kernel(u_ref, v_ref, g_ref, b_ref, ws_ref, bs_ref, o_ref, *, nblk, gd):
    row = lax.broadcasted_iota(jnp.int32, (GM_BLOCK, GM_BLOCK), 0) // CHUNK
    col = lax.broadcasted_iota(jnp.int32, (GM_BLOCK, GM_BLOCK), 1) // CHUNK
    keep = row >= col
    for g in range(GM_GROUPS):
        cols = slice(g * gd, (g + 1) * gd)
        wsg = jnp.where(keep, ws_ref[g], 0.0).astype(BF16)
        ln_g = g_ref[:, cols]
        ln_b = b_ref[:, cols]
        bsg = bs_ref[:, g:g + 1]
        for nb in range(nblk):
            rows = slice(nb * GM_BLOCK, (nb + 1) * GM_BLOCK)
            v = _gelu(v_ref[rows, cols].astype(F32))
            mu = jnp.mean(v, axis=-1, keepdims=True)
            d = v - mu
            var = jnp.mean(d * d, axis=-1, keepdims=True)
            vn = (d * lax.rsqrt(var + EPS) * ln_g + ln_b).astype(BF16)
            gate = _dot(wsg, vn) + bsg
            o_ref[rows, cols] = (_gelu(u_ref[rows, cols].astype(F32)) * gate).astype(o_ref.dtype)


def sgu(proj, ln_g, ln_b, ws, bs_t, width):
    m = proj.shape[0]
    gd = width // GM_GROUPS
    nblk = 2 if m % (2 * GM_BLOCK) == 0 else 1
    t = nblk * GM_BLOCK
    return pl.pallas_call(
        functools.partial(_sgu_kernel, nblk=nblk, gd=gd),
        grid=(m // t,),
        in_specs=[pl.BlockSpec((t, width), lambda i: (i, 0)),
                  pl.BlockSpec((t, width), lambda i: (i, 1)),
                  pl.BlockSpec((1, width), lambda i: (0, 0)),
                  pl.BlockSpec((1, width), lambda i: (0, 0)),
                  pl.BlockSpec((GM_GROUPS, GM_BLOCK, GM_BLOCK), lambda i: (0, 0, 0)),
                  pl.BlockSpec((GM_BLOCK, GM_GROUPS), lambda i: (0, 0))],
        out_specs=pl.BlockSpec((t, width), lambda i: (i, 0)),
        out_shape=jax.ShapeDtypeStruct((m, width), BF16),
        compiler_params=_params(("parallel",)),
        name="sgu",
    )(proj, proj, ln_g, ln_b, ws, bs_t)


def _ssd_kernel(z_ref, xbc_ref, dt_ref, cw_ref, cb_ref, dtb_ref, alog_ref, dsk_ref, nw_ref,
                e_ref, o_ref, xe_ref, xc_ref, cum_ref, cumx_ref, eac_ref, xdd_ref, xtop_ref,
                xbot_ref, bc_ref, y_ref, h_ref, *, t_rows, d_inner):
    L = CHUNK
    P = SSM_HEAD_DIM
    N = SSM_STATE
    G = SSM_GROUPS
    gw = d_inner // G
    pairs_per_group = gw // LANES
    conv_dim = d_inner + 2 * G * N
    cchunk = 512

    @pl.when(pl.program_id(1) == 0)
    def _():
        xe_ref[0:HALO, :] = jnp.zeros((HALO, conv_dim), F32)
        h_ref[...] = jnp.zeros_like(h_ref)

    @pl.when(pl.program_id(1) != 0)
    def _():
        xe_ref[0:HALO, :] = xe_ref[t_rows:t_rows + HALO, :]

    for c0 in range(0, conv_dim, cchunk):
        cs_ = slice(c0, c0 + cchunk)
        xe_ref[HALO:HALO + t_rows, cs_] = xbc_ref[:, cs_].astype(F32)
        xs = xe_ref[HALO - SUBLANES:HALO + t_rows, cs_]
        acc = cw_ref[0:1, cs_] * xs
        for k in range(1, SSM_CONV):
            acc = pltpu.roll(acc, 1, axis=0) + cw_ref[k:k + 1, cs_] * xs
        xc_ref[:, cs_] = _silu(acc[SUBLANES:, :] + cb_ref[:, cs_])

    nchunk = t_rows // L
    dtr = dt_ref[...] + dtb_ref[...]
    dt = jnp.maximum(dtr, 0.0) + jnp.log1p(jnp.exp(-jnp.abs(dtr)))
    a = -jnp.exp(alog_ref[...])
    rt = lax.broadcasted_iota(jnp.int32, (t_rows, t_rows), 0)
    ct = lax.broadcasted_iota(jnp.int32, (t_rows, t_rows), 1)
    tril = jnp.where((rt >= ct) & (rt // L == ct // L), 1.0, 0.0).astype(BF16)
    d_hi, d_mid, d_lo = _split3(dt * a)
    cum = _dot(tril, d_hi) + _dot(tril, d_mid) + _dot(tril, d_lo)
    cum_ref[...] = cum

    for c0 in range(0, d_inner, cchunk):
        cs_ = slice(c0, c0 + cchunk)
        e_c = e_ref[:, cs_]
        cum_x = _expand01(cum, e_c)
        xd = xc_ref[:, cs_] * _expand01(dt, e_c)
        last = jnp.concatenate(
            [jnp.broadcast_to(cum_x[c * L + L - 1:c * L + L, :], (L, cchunk))
             for c in range(nchunk)], axis=0)
        cumx_ref[:, cs_] = cum_x
        eac_ref[:, cs_] = jnp.exp(cum_x)
        xdd_ref[:, cs_] = (xd * jnp.exp(last - cum_x)).astype(BF16)
        first = (lax.broadcasted_iota(jnp.int32, (t_rows, cchunk), 1) % LANES) < P
        xtop_ref[:, cs_] = jnp.where(first, xd, 0.0).astype(BF16)
        xbot_ref[:, cs_] = jnp.where(first, 0.0, xd).astype(BF16)
    bc_ref[...] = xc_ref[:, d_inner:conv_dim].astype(BF16)

    ri2 = lax.broadcasted_iota(jnp.int32, (L, LANES), 0)
    ci2 = lax.broadcasted_iota(jnp.int32, (L, LANES), 1)
    lower2 = ri2 >= jnp.where(ci2 < P, ci2, ci2 - P)

    for c in range(nchunk):
        rows = slice(c * L, (c + 1) * L)
        cum_t = cum_ref[rows, :].T
        for g in range(G):
            gc = slice(g * gw, (g + 1) * gw)
            b_g = bc_ref[rows, g * N:(g + 1) * N]
            c_g = bc_ref[rows, G * N + g * N:G * N + (g + 1) * N]
            cb = _dot_nt(c_g, b_g)
            cb2 = jnp.concatenate([cb, cb], axis=1)
            eac_x = eac_ref[rows, gc]
            h_g = h_ref[g]
            y_off = _dot(c_g, h_g.astype(BF16)) * eac_x
            st = _dot_tn(b_g, xdd_ref[rows, gc])
            h_ref[g] = h_g * eac_x[L - 1:L, :] + st
            parts = []
            for q in range(pairs_per_group):
                hd = (g * pairs_per_group + q) * 2
                lanes = slice(g * gw + q * LANES, g * gw + (q + 1) * LANES)
                col_b = cumx_ref[rows, lanes]
                row_b = jnp.concatenate([cum_t[hd:hd + 1, :], cum_t[hd + 1:hd + 2, :]], axis=1)
                dec = jnp.exp(jnp.where(lower2, col_b - row_b, -jnp.inf))
                mm = (cb2 * dec).astype(BF16)
                rhs = jnp.concatenate([xtop_ref[rows, lanes], xbot_ref[rows, lanes]], axis=0)
                parts.append(_dot(mm, rhs))
            y_ref[rows, gc] = jnp.concatenate(parts, axis=1) + y_off

    for g in range(G):
        gc = slice(g * gw, (g + 1) * gw)
        y = y_ref[:, gc] + dsk_ref[:, gc] * xc_ref[:, gc]
        y = y * _silu(z_ref[:, gc].astype(F32))
        o_ref[:, gc] = _rms(y, nw_ref[:, gc]).astype(o_ref.dtype)


def ssd(proj, dt_raw, conv_w, conv_b, dt_bias, a_log, d_skip_x, norm_w, e_mat, seq, d_inner):
    m = proj.shape[0]
    conv_dim = d_inner + 2 * SSM_GROUPS * SSM_STATE
    assert 2 * SSM_HEAD_DIM == LANES
    t = _tile(seq, 256)
    spb = seq // t
    nb = m // seq
    z_blk = 2
    xbc_blk = (3 * d_inner) // conv_dim
    assert 3 * d_inner == xbc_blk * conv_dim
    kern = functools.partial(_ssd_kernel, t_rows=t, d_inner=d_inner)
    row = lambda b, s: (b * spb + s, 0)
    const = lambda b, s: (0, 0)
    return pl.pallas_call(
        kern,
        grid=(nb, spb),
        in_specs=[pl.BlockSpec((t, d_inner), lambda b, s: (b * spb + s, z_blk)),
                  pl.BlockSpec((t, conv_dim), lambda b, s: (b * spb + s, xbc_blk)),
                  pl.BlockSpec((t, LANES), row),
                  pl.BlockSpec((SSM_CONV, conv_dim), const),
                  pl.BlockSpec((1, conv_dim), const),
                  pl.BlockSpec((1, LANES), const),
                  pl.BlockSpec((1, LANES), const),
                  pl.BlockSpec((1, d_inner), const),
                  pl.BlockSpec((1, d_inner), const),
                  pl.BlockSpec((LANES, d_inner), const)],
        out_specs=pl.BlockSpec((t, d_inner), row),
        out_shape=jax.ShapeDtypeStruct((m, d_inner), BF16),
        scratch_shapes=[pltpu.VMEM((HALO + t, conv_dim), F32),
                        pltpu.VMEM((t, conv_dim), F32),
                        pltpu.VMEM((t, LANES), F32),
                        pltpu.VMEM((t, d_inner), F32),
                        pltpu.VMEM((t, d_inner), F32),
                        pltpu.VMEM((t, d_inner), BF16),
                        pltpu.VMEM((t, d_inner), BF16),
                        pltpu.VMEM((t, d_inner), BF16),
                        pltpu.VMEM((t, 2 * SSM_GROUPS * SSM_STATE), BF16),
                        pltpu.VMEM((t, d_inner), F32),
                        pltpu.VMEM((SSM_GROUPS, SSM_STATE, d_inner // SSM_GROUPS), F32)],
        compiler_params=_params(("arbitrary", "arbitrary")),
        name="ssd",
    )(proj, proj, dt_raw, conv_w, conv_b, dt_bias, a_log, d_skip_x, norm_w, e_mat)


def _mm_kernel(x_ref, w_ref, o_ref):
    o_ref[...] = _dot(x_ref[...], w_ref[...].astype(BF16)).astype(o_ref.dtype)


def matmul(x, w, out_dtype, tm_pref=1024, tn_pref=1024):
    m, k = x.shape
    n = w.shape[1]
    tm = _tile(m, tm_pref)
    tn = _tile(n, tn_pref)
    return pl.pallas_call(
        _mm_kernel,
        grid=(m // tm, n // tn),
        in_specs=[pl.BlockSpec((tm, k), lambda i, j: (i, 0)),
                  pl.BlockSpec((k, tn), lambda i, j: (0, j))],
        out_specs=pl.BlockSpec((tm, tn), lambda i, j: (i, j)),
        out_shape=jax.ShapeDtypeStruct((m, n), out_dtype),
        compiler_params=_params(("parallel", "parallel")),
        name="matmul",
    )(x, w)


def _cast_kernel(x_ref, o_ref):
    o_ref[...] = x_ref[...].astype(o_ref.dtype)


def cast_bf16(w_all):
    nl, k, n = w_all.shape
    tk = _tile(k, 512)
    tn = _tile(n, 2048)
    return pl.pallas_call(
        _cast_kernel,
        grid=(nl, k // tk, n // tn),
        in_specs=[pl.BlockSpec((None, tk, tn), lambda l, i, j: (l, i, j))],
        out_specs=pl.BlockSpec((None, tk, tn), lambda l, i, j: (l, i, j)),
        out_shape=jax.ShapeDtypeStruct((nl, k, n), BF16),
        compiler_params=_params(("parallel", "parallel", "parallel")),
        name="cast_bf16",
    )(w_all)


def _mm_res_kernel(x_ref, w_ref, r_ref, o_ref):
    o_ref[...] = r_ref[...] + _dot(x_ref[...], w_ref[...].astype(BF16))


def matmul_res(x, w_all, layer, res, tn_pref):
    m, k = x.shape
    n = w_all.shape[-1]
    tm = _tile(m, 1024)
    tn = _tile(n, tn_pref)
    return pl.pallas_call(
        _mm_res_kernel,
        grid=(m // tm, n // tn),
        in_specs=[pl.BlockSpec((tm, k), lambda i, j: (i, 0)),
                  pl.BlockSpec((None, k, tn), lambda i, j: (layer, 0, j)),
                  pl.BlockSpec((tm, tn), lambda i, j: (i, j))],
        out_specs=pl.BlockSpec((tm, tn), lambda i, j: (i, j)),
        out_shape=jax.ShapeDtypeStruct((m, n), F32),
        compiler_params=_params(("parallel", "parallel")),
        name="matmul_res",
    )(x, w_all, res)


def _mm2_res_kernel(xa_ref, xb_ref, wa_ref, wb_ref, r_ref, o_ref):
    o_ref[...] = (r_ref[...] + _dot(xa_ref[...], wa_ref[...].astype(BF16))
                  + _dot(xb_ref[...], wb_ref[...].astype(BF16)))


def matmul2_res(xa, xb, w_all, layer, res, tn_pref):
    m, k = xa.shape
    n = w_all.shape[-1]
    tm = _tile(m, 1024)
    tn = _tile(n, tn_pref)
    return pl.pallas_call(
        _mm2_res_kernel,
        grid=(m // tm, n // tn),
        in_specs=[pl.BlockSpec((tm, k), lambda i, j: (i, 0)),
                  pl.BlockSpec((tm, k), lambda i, j: (i, 0)),
                  pl.BlockSpec((None, k, tn), lambda i, j: (layer, 0, j)),
                  pl.BlockSpec((None, k, tn), lambda i, j: (layer, 1, j)),
                  pl.BlockSpec((tm, tn), lambda i, j: (i, j))],
        out_specs=pl.BlockSpec((tm, tn), lambda i, j: (i, j)),
        out_shape=jax.ShapeDtypeStruct((m, n), F32),
        compiler_params=_params(("parallel", "parallel")),
        name="matmul2_res",
    )(xa, xb, w_all, w_all, res)


def _rope_lanes(c2, cs):
    t = c2 * cs
    return t + pltpu.roll(t, MLA_ROPE, axis=1)


def _mla_inproj_kernel(x_ref, nw_ref, w_ref, qn_ref, kvn_ref, cs_ref, cq_ref, ckv_ref, kpe_ref):
    xn = _rms(x_ref[...], nw_ref[...]).astype(BF16)
    p = _dot(xn, w_ref[...])
    r = MLA_RANK
    cq_ref[...] = _rms(p[:, 0:r], qn_ref[...]).astype(BF16)
    ckv_ref[...] = _rms(p[:, r:2 * r], kvn_ref[...]).astype(BF16)
    kr = _rope_lanes(p[:, 2 * r:2 * r + LANES], cs_ref[...])
    lane = lax.broadcasted_iota(jnp.int32, kr.shape, 1)
    kpe_ref[...] = jnp.where(lane < MLA_ROPE, kr, 0.0).astype(BF16)


def mla_inproj(h, nw, w, qn, kvn, cs):
    m, k = h.shape
    n = w.shape[1]
    tm = _tile(m, 512)
    row = lambda i: (i, 0)
    const = lambda i: (0, 0)
    return pl.pallas_call(
        _mla_inproj_kernel,
        grid=(m // tm,),
        in_specs=[pl.BlockSpec((tm, k), row),
                  pl.BlockSpec((1, k), const),
                  pl.BlockSpec((k, n), const),
                  pl.BlockSpec((1, MLA_RANK), const),
                  pl.BlockSpec((1, MLA_RANK), const),
                  pl.BlockSpec((tm, LANES), row)],
        out_specs=[pl.BlockSpec((tm, MLA_RANK), row),
                   pl.BlockSpec((tm, MLA_RANK), row),
                   pl.BlockSpec((tm, LANES), row)],
        out_shape=[jax.ShapeDtypeStruct((m, MLA_RANK), BF16),
                   jax.ShapeDtypeStruct((m, MLA_RANK), BF16),
                   jax.ShapeDtypeStruct((m, LANES), BF16)],
        compiler_params=_params(("parallel",)),
        name="mla_inproj",
    )(h, nw, w, qn, kvn, cs)


def _q_up_kernel(x_ref, w_ref, cs_ref, o_ref, *, heads, scale):
    q = _dot(x_ref[...], w_ref[...])
    cs = cs_ref[...]
    hw = 2 * LANES
    for hh in range(heads):
        o_ref[:, hh * hw:hh * hw + LANES] = (q[:, hh * hw:hh * hw + LANES] * scale).astype(BF16)
        pe = _rope_lanes(q[:, hh * hw + LANES:(hh + 1) * hw], cs)
        o_ref[:, hh * hw + LANES:(hh + 1) * hw] = (pe * scale).astype(BF16)


def q_up(cq, w, cs, scale):
    m, k = cq.shape
    n = w.shape[1]
    tm = _tile(m, 512)
    tn = _tile(n, 1024)
    kern = functools.partial(_q_up_kernel, heads=tn // (2 * LANES), scale=scale)
    return pl.pallas_call(
        kern,
        grid=(m // tm, n // tn),
        in_specs=[pl.BlockSpec((tm, k), lambda i, j: (i, 0)),
                  pl.BlockSpec((k, tn), lambda i, j: (0, j)),
                  pl.BlockSpec((tm, LANES), lambda i, j: (i, 0))],
        out_specs=pl.BlockSpec((tm, tn), lambda i, j: (i, j)),
        out_shape=jax.ShapeDtypeStruct((m, n), BF16),
        compiler_params=_params(("parallel", "parallel")),
        name="q_up",
    )(cq, w, cs)


def _attn_kernel(q_ref, k_ref, kpe_ref, v_ref, o_ref, vt_ref, acc_ref, *, tq, tk, seq, hpg):
    i = pl.program_id(2)
    qw = 2 * LANES

    @pl.when(i == 0)
    def _():
        for h in range(hpg):
            for c in range(seq // tk):
                blk = v_ref[c * tk:(c + 1) * tk, h * MLA_V:(h + 1) * MLA_V].astype(F32)
                vt_ref[h, c] = blk.T.astype(BF16)

    acc_ref[...] = jnp.zeros(acc_ref.shape, F32)

    def scores(kb):
        ks = pl.multiple_of(kb * tk, tk)
        kpe = kpe_ref[pl.ds(ks, tk), :]
        out = []
        for h in range(hpg):
            kk = jnp.concatenate([k_ref[pl.ds(ks, tk), h * MLA_NOPE:(h + 1) * MLA_NOPE], kpe],
                                 axis=1)
            out.append(_dot_nt(kk, q_ref[:, h * qw:(h + 1) * qw]))
        return tuple(out)

    def update(kb, ms, ls, ss):
        new_ms, new_ls, alphas, ps = [], [], [], []
        for h in range(hpg):
            m_new = jnp.maximum(ms[h], jnp.max(ss[h], axis=0, keepdims=True))
            alpha = jnp.exp2(ms[h] - m_new)
            p = jnp.exp2(ss[h] - m_new)
            new_ls.append(alpha * ls[h] + jnp.sum(p, axis=0, keepdims=True))
            new_ms.append(m_new)
            alphas.append(alpha)
            ps.append(p.astype(BF16))
        pvs = [_dot(vt_ref[h, kb], ps[h]) for h in range(hpg)]
        for h in range(hpg):
            acc_ref[h] = alphas[h] * acc_ref[h] + pvs[h]
        return tuple(new_ms), tuple(new_ls)

    def body(kb, carry):
        ms, ls = carry
        return update(kb, ms, ls, scores(kb))

    init = (tuple(jnp.full((1, tq), -jnp.inf, F32) for _ in range(hpg)),
            tuple(jnp.zeros((1, tq), F32) for _ in range(hpg)))
    ndiag = (i * tq) // tk
    ms, ls = lax.fori_loop(0, ndiag, body, init)
    kc = lax.broadcasted_iota(jnp.int32, (tk, tq), 0) // CHUNK
    qc = lax.broadcasted_iota(jnp.int32, (tk, tq), 1) // CHUNK
    ss = tuple(jnp.where(kc <= qc, s, -jnp.inf) for s in scores(ndiag))
    _, ls = update(ndiag, ms, ls, ss)

    for h in range(hpg):
        o = acc_ref[h] * (1.0 / ls[h])
        o_ref[:, h * MLA_V:(h + 1) * MLA_V] = o.T.astype(o_ref.dtype)


def attention(q, kv, kpe, seq):
    m = q.shape[0]
    nb = m // seq
    tq = _tile(seq, 256)
    tk = tq
    nq = seq // tq
    hpg = ATTN_HEADS_PER_STEP
    ng = MLA_HEADS // hpg
    kern = functools.partial(_attn_kernel, tq=tq, tk=tk, seq=seq, hpg=hpg)
    return pl.pallas_call(
        kern,
        grid=(nb, ng, nq),
        in_specs=[pl.BlockSpec((tq, hpg * 2 * LANES), lambda b, g, i: (b * nq + i, g)),
                  pl.BlockSpec((seq, hpg * MLA_NOPE), lambda b, g, i: (b, g)),
                  pl.BlockSpec((seq, LANES), lambda b, g, i: (b, 0)),
                  pl.BlockSpec((seq, hpg * MLA_V), lambda b, g, i: (b, ng + g))],
        out_specs=pl.BlockSpec((tq, hpg * MLA_V), lambda b, g, i: (b * nq + i, g)),
        out_shape=jax.ShapeDtypeStruct((m, MLA_HEADS * MLA_V), BF16),
        scratch_shapes=[pltpu.VMEM((hpg, seq // tk, MLA_V, tk), BF16),
                        pltpu.VMEM((hpg, MLA_V, tq), F32)],
        compiler_params=_params(("parallel", "parallel", "arbitrary")),
        name="attention",
    )(q, kv, kpe, kv)


def _ffn_up_kernel(x_ref, nw_ref, wg_ref, wv_ref, cw_ref, cb_ref, o_ref, xn_ref, halo_ref,
                   *, tm, tiles_per_seq):
    i = pl.program_id(0)
    j = pl.program_id(1)

    @pl.when(j == 0)
    def _():
        xn_ref[...] = _rms(x_ref[...], nw_ref[...]).astype(BF16)

    @pl.when(i % tiles_per_seq == 0)
    def _():
        halo_ref[j] = jnp.zeros(halo_ref.shape[1:], F32)

    xn = xn_ref[...]
    g = _dot(xn, wg_ref[...].astype(BF16))
    val = _dot(xn, wv_ref[...].astype(BF16))
    gext = jnp.concatenate([halo_ref[j], g], axis=0)
    halo_ref[j] = g[tm - SUBLANES:tm, :]
    acc = cw_ref[0:1, :] * gext
    for k in range(1, FFN_CONV):
        acc = pltpu.roll(acc, 1, axis=0) + cw_ref[k:k + 1, :] * gext
    conv = acc[SUBLANES:, :] + cb_ref[...]
    o_ref[...] = (_gelu(conv) * val).astype(o_ref.dtype)


def ffn_up(h, nw, w_up_all, conv_w_all, conv_b_all, layer, seq):
    m, k = h.shape
    dff = w_up_all.shape[-1] // 2
    tm = _tile(seq, 1024)
    tn = _tile(dff, 512)
    nj = dff // tn
    kern = functools.partial(_ffn_up_kernel, tm=tm, tiles_per_seq=seq // tm)
    return pl.pallas_call(
        kern,
        grid=(m // tm, nj),
        in_specs=[pl.BlockSpec((tm, k), lambda i, j: (i, 0)),
                  pl.BlockSpec((1, k), lambda i, j: (0, 0)),
                  pl.BlockSpec((None, k, tn), lambda i, j: (layer, 0, j)),
                  pl.BlockSpec((None, k, tn), lambda i, j: (layer, 0, nj + j)),
                  pl.BlockSpec((None, FFN_CONV, tn), lambda i, j: (layer, 0, j)),
                  pl.BlockSpec((None, 1, tn), lambda i, j: (layer, 0, j))],
        out_specs=pl.BlockSpec((tm, tn), lambda i, j: (i, j)),
        out_shape=jax.ShapeDtypeStruct((m, dff), BF16),
        scratch_shapes=[pltpu.VMEM((tm, k), BF16),
                        pltpu.VMEM((nj, SUBLANES, tn), F32)],
        compiler_params=_params(("arbitrary", "arbitrary")),
        name="ffn_up",
    )(h, nw, w_up_all, w_up_all, conv_w_all, conv_b_all)


def _rmsnorm_kernel(x_ref, w_ref, o_ref):
    o_ref[...] = _rms(x_ref[...], w_ref[...])


def rmsnorm(x, w):
    m, k = x.shape
    tm = _tile(m, 512)
    return pl.pallas_call(
        _rmsnorm_kernel,
        grid=(m // tm,),
        in_specs=[pl.BlockSpec((tm, k), lambda i: (i, 0)),
                  pl.BlockSpec((1, k), lambda i: (0, 0))],
        out_specs=pl.BlockSpec((tm, k), lambda i: (i, 0)),
        out_shape=jax.ShapeDtypeStruct((m, k), F32),
        compiler_params=_params(("parallel",)),
        name="final_rmsnorm",
    )(x, w)


def _rot_half_cols(w):
    half = w.shape[-1] // 2
    return jnp.concatenate([-w[..., half:], w[..., :half]], axis=-1)


def _pad_lanes(v):
    pad = (-v.shape[-1]) % LANES
    return jnp.pad(v, [(0, 0)] * (v.ndim - 1) + [(0, pad)])


def _even_layer(h, seq, nw, j, w_in_main, w_in_t, ln_g, ln_b, ws, bs, conv_w, conv_b, dt_bias,
                a_log, d_skip, ssm_nw, ev_w_out):
    width = h.shape[1]
    proj, dt_raw = even_inproj(h, nw[None, :], w_in_main, w_in_t, j)
    ya = sgu(proj, ln_g.reshape(1, width), ln_b.reshape(1, width), ws, bs.T, width)
    head_of_col = jnp.arange(width) // SSM_HEAD_DIM
    e_mat = (jnp.arange(LANES)[:, None] == head_of_col[None, :]).astype(BF16)
    yb = ssd(proj, dt_raw, conv_w, conv_b[None, :], _pad_lanes(dt_bias[None, :]),
             _pad_lanes(a_log[None, :]), jnp.repeat(d_skip, SSM_HEAD_DIM)[None, :],
             ssm_nw[None, :], e_mat, seq, width)
    return matmul2_res(ya, yb, ev_w_out, j, h, 512)


def _odd_layer(h, seq, cs, nw, j, w_in, qn, kvn, w_uq, w_ukv, od_w_o):
    r = MLA_RANK
    qk = MLA_NOPE + MLA_ROPE
    w_kr = w_in[:, 2 * r:]
    w_in2 = jnp.concatenate([w_in, _rot_half_cols(w_kr)], axis=1).astype(BF16)
    cq, ckv, kpe = mla_inproj(h, nw[None, :], w_in2, qn[None, :], kvn[None, :], cs)
    wq = w_uq.reshape(r, MLA_HEADS, qk)
    wq_pe = wq[:, :, MLA_NOPE:]
    wq2 = jnp.concatenate([wq[:, :, :MLA_NOPE], wq_pe, _rot_half_cols(wq_pe)], axis=-1)
    wq2 = wq2.reshape(r, MLA_HEADS * 2 * LANES).astype(BF16)
    q = q_up(cq, wq2, cs, qk ** -0.5 * LOG2E)
    wkv = w_ukv.reshape(r, MLA_HEADS, MLA_NOPE + MLA_V)
    wkv2 = jnp.concatenate([wkv[:, :, :MLA_NOPE].reshape(r, MLA_HEADS * MLA_NOPE),
                            wkv[:, :, MLA_NOPE:].reshape(r, MLA_HEADS * MLA_V)], axis=1)
    kv = matmul(ckv, wkv2.astype(BF16), BF16)
    o = attention(q, kv, kpe, seq)
    return matmul_res(o, od_w_o, j, h, 1024)


def kernel(x, positions, norm_mix, norm_ffn, norm_final, ev_w_in, ev_gm_ln_g, ev_gm_ln_b, ev_gm_ws, ev_gm_bs, ev_conv_w, ev_conv_b, ev_dt_bias, ev_a_log, ev_d_skip, ev_ssm_norm_w, ev_w_out, od_w_in, od_q_norm, od_kv_norm, od_w_uq, od_w_ukv, od_w_o, ff_w_up, ff_conv_w, ff_conv_b, ff_w_down):
    bsz, seq, d = x.shape
    m = bsz * seq
    depth = norm_mix.shape[0]

    inv_freq = ROPE_THETA ** (-jnp.arange(0, MLA_ROPE, 2, dtype=F32) / MLA_ROPE)
    freq_row = jnp.tile(inv_freq, 4)[None, :]
    pos_b = jnp.broadcast_to(positions.astype(F32).reshape(m, 1), (m, LANES))
    cs = rope_table(pos_b, freq_row)
    ff_conv_b3 = ff_conv_b[:, None, :]
    w_down_bf16 = cast_bf16(ff_w_down)
    ff_w_up = cast_bf16(ff_w_up)
    ev_w_out = cast_bf16(ev_w_out)
    od_w_o = cast_bf16(od_w_o)
    w_in_t = jnp.swapaxes(ev_w_in, 1, 2)
    n_main = 3 * d + (d + 2 * SSM_GROUPS * SSM_STATE)
    w_in_main = transpose_cast(w_in_t, n_main)

    h = x.reshape(m, d)
    for layer in range(depth):
        j = layer // 2
        if layer % 2 == 0:
            h = _even_layer(h, seq, norm_mix[layer], j, w_in_main, w_in_t, ev_gm_ln_g[j],
                            ev_gm_ln_b[j],
                            ev_gm_ws[j], ev_gm_bs[j], ev_conv_w[j], ev_conv_b[j], ev_dt_bias[j],
                            ev_a_log[j], ev_d_skip[j], ev_ssm_norm_w[j], ev_w_out)
        else:
            h = _odd_layer(h, seq, cs, norm_mix[layer], j, od_w_in[j], od_q_norm[j],
                           od_kv_norm[j], od_w_uq[j], od_w_ukv[j], od_w_o)
        act = ffn_up(h, norm_ffn[layer][None, :], ff_w_up, ff_conv_w, ff_conv_b3, layer, seq)
        h = matmul_res(act, w_down_bf16, layer, h, 512)
    return rmsnorm(h, norm_final[None, :]).reshape(bsz, seq, d)
```

```python
import functools

import jax
import jax.numpy as jnp
from jax import lax
from jax.experimental import pallas as pl
from jax.experimental.pallas import tpu as pltpu

F32 = jnp.float32
BF16 = jnp.bfloat16

EPS = 1e-6
LANES = 128
SUBLANES = 8
HALO = 16
VMEM_LIMIT = 56 * 1024 * 1024

CHUNK = 64
GM_BLOCK = 128
GM_GROUPS = 8
SSM_HEAD_DIM = 64
SSM_GROUPS = 4
SSM_STATE = 128
SSM_CONV = 4
MLA_HEADS = 16
MLA_NOPE = 128
MLA_ROPE = 64
MLA_V = 128
MLA_RANK = 512
ATTN_HEADS_PER_STEP = 8
ROPE_THETA = 10000.0
LOG2E = 1.4426950408889634
FFN_CONV = 3


def _params(sem):
    return pltpu.CompilerParams(dimension_semantics=sem, vmem_limit_bytes=VMEM_LIMIT)


def _dot(a, b):
    return jnp.dot(a, b, preferred_element_type=F32)


def _dot_nt(a, b):
    return lax.dot_general(a, b, (((1,), (1,)), ((), ())), preferred_element_type=F32)


def _dot_tn(a, b):
    return lax.dot_general(a, b, (((0,), (0,)), ((), ())), preferred_element_type=F32)


def _rms(x, w):
    ms = jnp.mean(x * x, axis=-1, keepdims=True)
    return x * lax.rsqrt(ms + EPS) * w


def _gelu(x):
    return 0.5 * x * (1.0 + jnp.tanh(0.7978845608028654 * (x + 0.044715 * (x * x * x))))


def _silu(x):
    hx = 0.5 * x
    return hx + hx * jnp.tanh(hx)


def _split3(x):
    hi = x.astype(BF16)
    r1 = x - hi.astype(F32)
    mid = r1.astype(BF16)
    lo = (r1 - mid.astype(F32)).astype(BF16)
    return hi, mid, lo


def _expand01(x, sel):
    hi, mid, lo = _split3(x)
    return _dot(hi, sel) + _dot(mid, sel) + _dot(lo, sel)


def _tile(n, pref):
    if n <= pref:
        return n
    t = pref
    while t >= LANES:
        if n % t == 0:
            return t
        t -= LANES
    return n


def _rope_table_kernel(pos_ref, freq_ref, o_ref):
    ang = pos_ref[...] * freq_ref[...]
    lane = lax.broadcasted_iota(jnp.int32, ang.shape, 1)
    o_ref[...] = jnp.where(lane < MLA_ROPE, jnp.cos(ang), jnp.sin(ang))


def rope_table(pos_b, freq_row):
    m = pos_b.shape[0]
    tm = _tile(m, 1024)
    return pl.pallas_call(
        _rope_table_kernel,
        grid=(m // tm,),
        in_specs=[pl.BlockSpec((tm, LANES), lambda i: (i, 0)),
                  pl.BlockSpec((1, LANES), lambda i: (0, 0))],
        out_specs=pl.BlockSpec((tm, LANES), lambda i: (i, 0)),
        out_shape=jax.ShapeDtypeStruct((m, LANES), F32),
        compiler_params=_params(("parallel",)),
        name="rope_table",
    )(pos_b, freq_row)


def _transpose_cast_kernel(x_ref, o_ref):
    o_ref[...] = x_ref[...].T.astype(o_ref.dtype)


def transpose_cast(wt_all, n_main):
    nl, _, k = wt_all.shape
    tn = _tile(n_main, 512)
    return pl.pallas_call(
        _transpose_cast_kernel,
        grid=(nl, n_main // tn),
        in_specs=[pl.BlockSpec((None, tn, k), lambda l, j: (l, j, 0))],
        out_specs=pl.BlockSpec((None, k, tn), lambda l, j: (l, 0, j)),
        out_shape=jax.ShapeDtypeStruct((nl, k, n_main), BF16),
        compiler_params=_params(("parallel", "parallel")),
        name="transpose_cast",
    )(wt_all)


def _even_inproj_kernel(x_ref, nw_ref, w_ref, wdt_ref, o_ref, dt_ref, xn_ref):
    @pl.when(pl.program_id(1) == 0)
    def _():
        xn = _rms(x_ref[...], nw_ref[...]).astype(BF16)
        xn_ref[...] = xn
        n_dt, k = wdt_ref.shape
        wdt = jnp.concatenate([wdt_ref[...], jnp.zeros((LANES - n_dt, k), F32)], axis=0)
        dt_ref[...] = _dot_nt(xn, wdt.astype(BF16))

    o_ref[...] = _dot(xn_ref[...], w_ref[...]).astype(o_ref.dtype)


def even_inproj(h, nw, w_main, wt_all, layer):
    m, k = h.shape
    n_main = w_main.shape[-1]
    n_dt = wt_all.shape[1] - n_main
    assert 0 < n_dt < LANES and n_dt % SUBLANES == 0 and n_main % n_dt == 0
    tm = _tile(m, 1024)
    tn = _tile(n_main, 1024)
    return pl.pallas_call(
        _even_inproj_kernel,
        grid=(m // tm, n_main // tn),
        in_specs=[pl.BlockSpec((tm, k), lambda i, j: (i, 0)),
                  pl.BlockSpec((1, k), lambda i, j: (0, 0)),
                  pl.BlockSpec((None, k, tn), lambda i, j: (layer, 0, j)),
                  pl.BlockSpec((None, n_dt, k), lambda i, j: (layer, n_main // n_dt, 0))],
        out_specs=[pl.BlockSpec((tm, tn), lambda i, j: (i, j)),
                   pl.BlockSpec((tm, LANES), lambda i, j: (i, 0))],
        out_shape=[jax.ShapeDtypeStruct((m, n_main), BF16),
                   jax.ShapeDtypeStruct((m, LANES), F32)],
        scratch_shapes=[pltpu.VMEM((tm, k), BF16)],
        compiler_params=_params(("parallel", "arbitrary")),
        name="even_inproj",
    )(h, nw, w_main, wt_all)


def _sgu_kernel(u_ref, v_ref, g_ref, b_ref, ws_ref, bs_ref, o_ref, *, nblk, gd):
    row = lax.broadcasted_iota(jnp.int32, (GM_BLOCK, GM_BLOCK), 0) // CHUNK
    col = lax.broadcasted_iota(jnp.int32, (GM_BLOCK, GM_BLOCK), 1) // CHUNK
    keep = row >= col
    for g in range(GM_GROUPS):
        cols = slice(g * gd, (g + 1) * gd)
        wsg = jnp.where(keep, ws_ref[g], 0.0).astype(BF16)
        ln_g = g_ref[:, cols]
        ln_b = b_ref[:, cols]
        bsg = bs_ref[:, g:g + 1]
        for nb in range(nblk):
            rows = slice(nb * GM_BLOCK, (nb + 1) * GM_BLOCK)
            v = _gelu(v_ref[rows, cols].astype(F32))
            mu = jnp.mean(v, axis=-1, keepdims=True)
            d = v - mu
            var = jnp.mean(d * d, axis=-1, keepdims=True)
            vn = (d * lax.rsqrt(var + EPS) * ln_g + ln_b).astype(BF16)
            gate = _dot(wsg, vn) + bsg
            o_ref[rows, cols] = (_gelu(u_ref[rows, cols].astype(F32)) * gate).astype(o_ref.dtype)


def sgu(proj, ln_g, ln_b, ws, bs_t, width):
    m = proj.shape[0]
    gd = width // GM_GROUPS
    nblk = 2 if m % (2 * GM_BLOCK) == 0 else 1
    t = nblk * GM_BLOCK
    return pl.pallas_call(
        functools.partial(_sgu_kernel, nblk=nblk, gd=gd),
        grid=(m // t,),
        in_specs=[pl.BlockSpec((t, width), lambda i: (i, 0)),
                  pl.BlockSpec((t, width), lambda i: (i, 1)),
                  pl.BlockSpec((1, width), lambda i: (0, 0)),
                  pl.BlockSpec((1, width), lambda i: (0, 0)),
                  pl.BlockSpec((GM_GROUPS, GM_BLOCK, GM_BLOCK), lambda i: (0, 0, 0)),
                  pl.BlockSpec((GM_BLOCK, GM_GROUPS), lambda i: (0, 0))],
        out_specs=pl.BlockSpec((t, width), lambda i: (i, 0)),
        out_shape=jax.ShapeDtypeStruct((m, width), BF16),
        compiler_params=_params(("parallel",)),
        name="sgu",
    )(proj, proj, ln_g, ln_b, ws, bs_t)


def _ssd_kernel(z_ref, xbc_ref, dt_ref, cw_ref, cb_ref, dtb_ref, alog_ref, dsk_ref, nw_ref,
                e_ref, o_ref, xe_ref, xc_ref, cum_ref, cumx_ref, eac_ref, xdd_ref, xtop_ref,
                xbot_ref, bc_ref, y_ref, h_ref, *, t_rows, d_inner):
    L = CHUNK
    P = SSM_HEAD_DIM
    N = SSM_STATE
    G = SSM_GROUPS
    gw = d_inner // G
    pairs_per_group = gw // LANES
    conv_dim = d_inner + 2 * G * N
    cchunk = 512

    @pl.when(pl.program_id(1) == 0)
    def _():
        xe_ref[0:HALO, :] = jnp.zeros((HALO, conv_dim), F32)
        h_ref[...] = jnp.zeros_like(h_ref)

    @pl.when(pl.program_id(1) != 0)
    def _():
        xe_ref[0:HALO, :] = xe_ref[t_rows:t_rows + HALO, :]

    for c0 in range(0, conv_dim, cchunk):
        cs_ = slice(c0, c0 + cchunk)
        xe_ref[HALO:HALO + t_rows, cs_] = xbc_ref[:, cs_].astype(F32)
        xs = xe_ref[HALO - SUBLANES:HALO + t_rows, cs_]
        acc = cw_ref[0:1, cs_] * xs
        for k in range(1, SSM_CONV):
            acc = pltpu.roll(acc, 1, axis=0) + cw_ref[k:k + 1, cs_] * xs
        xc_ref[:, cs_] = _silu(acc[SUBLANES:, :] + cb_ref[:, cs_])

    nchunk = t_rows // L
    dtr = dt_ref[...] + dtb_ref[...]
    dt = jnp.maximum(dtr, 0.0) + jnp.log1p(jnp.exp(-jnp.abs(dtr)))
    a = -jnp.exp(alog_ref[...])
    rt = lax.broadcasted_iota(jnp.int32, (t_rows, t_rows), 0)
    ct = lax.broadcasted_iota(jnp.int32, (t_rows, t_rows), 1)
    tril = jnp.where((rt >= ct) & (rt // L == ct // L), 1.0, 0.0).astype(BF16)
    d_hi, d_mid, d_lo = _split3(dt * a)
    cum = _dot(tril, d_hi) + _dot(tril, d_mid) + _dot(tril, d_lo)
    cum_ref[...] = cum

    for c0 in range(0, d_inner, cchunk):
        cs_ = slice(c0, c0 + cchunk)
        e_c = e_ref[:, cs_]
        cum_x = _expand01(cum, e_c)
        xd = xc_ref[:, cs_] * _expand01(dt, e_c)
        last = jnp.concatenate(
            [jnp.broadcast_to(cum_x[c * L + L - 1:c * L + L, :], (L, cchunk))
             for c in range(nchunk)], axis=0)
        cumx_ref[:, cs_] = cum_x
        eac_ref[:, cs_] = jnp.exp(cum_x)
        xdd_ref[:, cs_] = (xd * jnp.exp(last - cum_x)).astype(BF16)
        first = (lax.broadcasted_iota(jnp.int32, (t_rows, cchunk), 1) % LANES) < P
        xtop_ref[:, cs_] = jnp.where(first, xd, 0.0).astype(BF16)
        xbot_ref[:, cs_] = jnp.where(first, 0.0, xd).astype(BF16)
    bc_ref[...] = xc_ref[:, d_inner:conv_dim].astype(BF16)

    ri2 = lax.broadcasted_iota(jnp.int32, (L, LANES), 0)
    ci2 = lax.broadcasted_iota(jnp.int32, (L, LANES), 1)
    lower2 = ri2 >= jnp.where(ci2 < P, ci2, ci2 - P)

    for c in range(nchunk):
        rows = slice(c * L, (c + 1) * L)
        cum_t = cum_ref[rows, :].T
        for g in range(G):
            gc = slice(g * gw, (g + 1) * gw)
            b_g = bc_ref[rows, g * N:(g + 1) * N]
            c_g = bc_ref[rows, G * N + g * N:G * N + (g + 1) * N]
            cb = _dot_nt(c_g, b_g)
            cb2 = jnp.concatenate([cb, cb], axis=1)
            eac_x = eac_ref[rows, gc]
            h_g = h_ref[g]
            y_off = _dot(c_g, h_g.astype(BF16)) * eac_x
            st = _dot_tn(b_g, xdd_ref[rows, gc])
            h_ref[g] = h_g * eac_x[L - 1:L, :] + st
            parts = []
            for q in range(pairs_per_group):
                hd = (g * pairs_per_group + q) * 2
                lanes = slice(g * gw + q * LANES, g * gw + (q + 1) * LANES)
                col_b = cumx_ref[rows, lanes]
                row_b = jnp.concatenate([cum_t[hd:hd + 1, :], cum_t[hd + 1:hd + 2, :]], axis=1)
                dec = jnp.exp(jnp.where(lower2, col_b - row_b, -jnp.inf))
                mm = (cb2 * dec).astype(BF16)
                rhs = jnp.concatenate([xtop_ref[rows, lanes], xbot_ref[rows, lanes]], axis=0)
                parts.append(_dot(mm, rhs))
            y_ref[rows, gc] = jnp.concatenate(parts, axis=1) + y_off

    for g in range(G):
        gc = slice(g * gw, (g + 1) * gw)
        y = y_ref[:, gc] + dsk_ref[:, gc] * xc_ref[:, gc]
        y = y * _silu(z_ref[:, gc].astype(F32))
        o_ref[:, gc] = _rms(y, nw_ref[:, gc]).astype(o_ref.dtype)


def ssd(proj, dt_raw, conv_w, conv_b, dt_bias, a_log, d_skip_x, norm_w, e_mat, seq, d_inner):
    m = proj.shape[0]
    conv_dim = d_inner + 2 * SSM_GROUPS * SSM_STATE
    assert 2 * SSM_HEAD_DIM == LANES
    t = _tile(seq, 256)
    spb = seq // t
    nb = m // seq
    z_blk = 2
    xbc_blk = (3 * d_inner) // conv_dim
    assert 3 * d_inner == xbc_blk * conv_dim
    kern = functools.partial(_ssd_kernel, t_rows=t, d_inner=d_inner)
    row = lambda b, s: (b * spb + s, 0)
    const = lambda b, s: (0, 0)
    return pl.pallas_call(
        kern,
        grid=(nb, spb),
        in_specs=[pl.BlockSpec((t, d_inner), lambda b, s: (b * spb + s, z_blk)),
                  pl.BlockSpec((t, conv_dim), lambda b, s: (b * spb + s, xbc_blk)),
                  pl.BlockSpec((t, LANES), row),
                  pl.BlockSpec((SSM_CONV, conv_dim), const),
                  pl.BlockSpec((1, conv_dim), const),
                  pl.BlockSpec((1, LANES), const),
                  pl.BlockSpec((1, LANES), const),
                  pl.BlockSpec((1, d_inner), const),
                  pl.BlockSpec((1, d_inner), const),
                  pl.BlockSpec((LANES, d_inner), const)],
        out_specs=pl.BlockSpec((t, d_inner), row),
        out_shape=jax.ShapeDtypeStruct((m, d_inner), BF16),
        scratch_shapes=[pltpu.VMEM((HALO + t, conv_dim), F32),
                        pltpu.VMEM((t, conv_dim), F32),
                        pltpu.VMEM((t, LANES), F32),
                        pltpu.VMEM((t, d_inner), F32),
                        pltpu.VMEM((t, d_inner), F32),
                        pltpu.VMEM((t, d_inner), BF16),
                        pltpu.VMEM((t, d_inner), BF16),
                        pltpu.VMEM((t, d_inner), BF16),
                        pltpu.VMEM((t, 2 * SSM_GROUPS * SSM_STATE), BF16),
                        pltpu.VMEM((t, d_inner), F32),
                        pltpu.VMEM((SSM_GROUPS, SSM_STATE, d_inner // SSM_GROUPS), F32)],
        compiler_params=_params(("arbitrary", "arbitrary")),
        name="ssd",
    )(proj, proj, dt_raw, conv_w, conv_b, dt_bias, a_log, d_skip_x, norm_w, e_mat)


def _cast_kernel(x_ref, o_ref):
    o_ref[...] = x_ref[...].astype(o_ref.dtype)


def cast_bf16(w_all):
    nl, k, n = w_all.shape
    tk = _tile(k, 512)
    tn = _tile(n, 2048)
    return pl.pallas_call(
        _cast_kernel,
        grid=(nl, k // tk, n // tn),
        in_specs=[pl.BlockSpec((None, tk, tn), lambda l, i, j: (l, i, j))],
        out_specs=pl.BlockSpec((None, tk, tn), lambda l, i, j: (l, i, j)),
        out_shape=jax.ShapeDtypeStruct((nl, k, n), BF16),
        compiler_params=_params(("parallel", "parallel", "parallel")),
        name="cast_bf16",
    )(w_all)


def _mm_res_kernel(x_ref, w_ref, r_ref, o_ref):
    o_ref[...] = r_ref[...] + _dot(x_ref[...], w_ref[...].astype(BF16))


def matmul_res(x, w_all, layer, res, tn_pref):
    m, k = x.shape
    n = w_all.shape[-1]
    tm = _tile(m, 1024)
    tn = _tile(n, tn_pref)
    return pl.pallas_call(
        _mm_res_kernel,
        grid=(m // tm, n // tn),
        in_specs=[pl.BlockSpec((tm, k), lambda i, j: (i, 0)),
                  pl.BlockSpec((None, k, tn), lambda i, j: (layer, 0, j)),
                  pl.BlockSpec((tm, tn), lambda i, j: (i, j))],
        out_specs=pl.BlockSpec((tm, tn), lambda i, j: (i, j)),
        out_shape=jax.ShapeDtypeStruct((m, n), F32),
        compiler_params=_params(("parallel", "parallel")),
        name="matmul_res",
    )(x, w_all, res)


def _mm2_res_kernel(xa_ref, xb_ref, wa_ref, wb_ref, r_ref, o_ref):
    o_ref[...] = (r_ref[...] + _dot(xa_ref[...], wa_ref[...].astype(BF16))
                  + _dot(xb_ref[...], wb_ref[...].astype(BF16)))


def matmul2_res(xa, xb, w_all, layer, res, tn_pref):
    m, k = xa.shape
    n = w_all.shape[-1]
    tm = _tile(m, 1024)
    tn = _tile(n, tn_pref)
    return pl.pallas_call(
        _mm2_res_kernel,
        grid=(m // tm, n // tn),
        in_specs=[pl.BlockSpec((tm, k), lambda i, j: (i, 0)),
                  pl.BlockSpec((tm, k), lambda i, j: (i, 0)),
                  pl.BlockSpec((None, k, tn), lambda i, j: (layer, 0, j)),
                  pl.BlockSpec((None, k, tn), lambda i, j: (layer, 1, j)),
                  pl.BlockSpec((tm, tn), lambda i, j: (i, j))],
        out_specs=pl.BlockSpec((tm, tn), lambda i, j: (i, j)),
        out_shape=jax.ShapeDtypeStruct((m, n), F32),
        compiler_params=_params(("parallel", "parallel")),
        name="matmul2_res",
    )(xa, xb, w_all, w_all, res)


def _rope_lanes(c2, cs):
    t = c2 * cs
    return t + pltpu.roll(t, MLA_ROPE, axis=1)


def _mla_proj_kernel(x_ref, nw_ref, win_ref, qn_ref, kvn_ref, wq_ref, wkv_ref, cs_ref,
                     q_ref, kv_ref, kpe_ref, *, scale, group):
    xn = _rms(x_ref[...], nw_ref[...]).astype(BF16)
    p = _dot(xn, win_ref[...])
    r = MLA_RANK
    cq = _rms(p[:, 0:r], qn_ref[...]).astype(BF16)
    ckv = _rms(p[:, r:2 * r], kvn_ref[...]).astype(BF16)
    cs = cs_ref[...]
    kr = _rope_lanes(p[:, 2 * r:2 * r + LANES], cs)
    lane = lax.broadcasted_iota(jnp.int32, kr.shape, 1)
    kpe_ref[...] = jnp.where(lane < MLA_ROPE, kr, 0.0).astype(BF16)

    hw = 2 * LANES
    for c0 in range(0, kv_ref.shape[1], group * hw):
        cols = slice(c0, c0 + group * hw)
        kv_ref[:, cols] = _dot(ckv, wkv_ref[:, cols]).astype(BF16)
        q = _dot(cq, wq_ref[:, cols])
        for hh in range(group):
            lo = c0 + hh * hw
            q_ref[:, lo:lo + LANES] = (q[:, hh * hw:hh * hw + LANES] * scale).astype(BF16)
            pe = _rope_lanes(q[:, hh * hw + LANES:(hh + 1) * hw], cs)
            q_ref[:, lo + LANES:lo + hw] = (pe * scale).astype(BF16)


def mla_proj(h, nw, w_in, qn, kvn, wq, wkv, cs, scale):
    m, k = h.shape
    nq = wq.shape[1]
    assert wkv.shape[1] == nq
    tm = _tile(m, 512)
    row = lambda i: (i, 0)
    const = lambda i: (0, 0)
    resident = pl.Buffered(1)
    return pl.pallas_call(
        functools.partial(_mla_proj_kernel, scale=scale, group=4),
        grid=(m // tm,),
        in_specs=[pl.BlockSpec((tm, k), row),
                  pl.BlockSpec((1, k), const),
                  pl.BlockSpec(w_in.shape, const, pipeline_mode=resident),
                  pl.BlockSpec((1, MLA_RANK), const),
                  pl.BlockSpec((1, MLA_RANK), const),
                  pl.BlockSpec(wq.shape, const, pipeline_mode=resident),
                  pl.BlockSpec(wkv.shape, const, pipeline_mode=resident),
                  pl.BlockSpec((tm, LANES), row)],
        out_specs=[pl.BlockSpec((tm, nq), row),
                   pl.BlockSpec((tm, nq), row),
                   pl.BlockSpec((tm, LANES), row)],
        out_shape=[jax.ShapeDtypeStruct((m, nq), BF16),
                   jax.ShapeDtypeStruct((m, nq), BF16),
                   jax.ShapeDtypeStruct((m, LANES), BF16)],
        compiler_params=_params(("parallel",)),
        name="mla_proj",
    )(h, nw, w_in, qn, kvn, wq, wkv, cs)


def _attn_kernel(q_ref, k_ref, kpe_ref, v_ref, o_ref, vt_ref, acc_ref, *, tq, tk, seq, hpg):
    i = pl.program_id(2)
    qw = 2 * LANES

    @pl.when(i == 0)
    def _():
        for h in range(hpg):
            for c in range(seq // tk):
                blk = v_ref[c * tk:(c + 1) * tk, h * MLA_V:(h + 1) * MLA_V].astype(F32)
                vt_ref[h, c] = blk.T.astype(BF16)

    acc_ref[...] = jnp.zeros(acc_ref.shape, F32)

    def scores(kb):
        ks = pl.multiple_of(kb * tk, tk)
        kpe = kpe_ref[pl.ds(ks, tk), :]
        out = []
        for h in range(hpg):
            kk = jnp.concatenate([k_ref[pl.ds(ks, tk), h * MLA_NOPE:(h + 1) * MLA_NOPE], kpe],
                                 axis=1)
            out.append(_dot_nt(kk, q_ref[:, h * qw:(h + 1) * qw]))
        return tuple(out)

    def update(kb, ms, ls, ss):
        new_ms, new_ls, alphas, ps = [], [], [], []
        for h in range(hpg):
            m_new = jnp.maximum(ms[h], jnp.max(ss[h], axis=0, keepdims=True))
            alpha = jnp.exp2(ms[h] - m_new)
            p = jnp.exp2(ss[h] - m_new)
            new_ls.append(alpha * ls[h] + jnp.sum(p, axis=0, keepdims=True))
            new_ms.append(m_new)
            alphas.append(alpha)
            ps.append(p.astype(BF16))
        pvs = [_dot(vt_ref[h, kb], ps[h]) for h in range(hpg)]
        for h in range(hpg):
            acc_ref[h] = alphas[h] * acc_ref[h] + pvs[h]
        return tuple(new_ms), tuple(new_ls)

    def body(kb, carry):
        ms, ls = carry
        return update(kb, ms, ls, scores(kb))

    init = (tuple(jnp.full((1, tq), -jnp.inf, F32) for _ in range(hpg)),
            tuple(jnp.zeros((1, tq), F32) for _ in range(hpg)))
    ndiag = (i * tq) // tk
    ms, ls = lax.fori_loop(0, ndiag, body, init)
    kc = lax.broadcasted_iota(jnp.int32, (tk, tq), 0) // CHUNK
    qc = lax.broadcasted_iota(jnp.int32, (tk, tq), 1) // CHUNK
    ss = tuple(jnp.where(kc <= qc, s, -jnp.inf) for s in scores(ndiag))
    _, ls = update(ndiag, ms, ls, ss)

    for h in range(hpg):
        o = acc_ref[h] * (1.0 / ls[h])
        o_ref[:, h * MLA_V:(h + 1) * MLA_V] = o.T.astype(o_ref.dtype)


def attention(q, kv, kpe, seq):
    m = q.shape[0]
    nb = m // seq
    tq = _tile(seq, 256)
    tk = tq
    nq = seq // tq
    hpg = ATTN_HEADS_PER_STEP
    ng = MLA_HEADS // hpg
    kern = functools.partial(_attn_kernel, tq=tq, tk=tk, seq=seq, hpg=hpg)
    return pl.pallas_call(
        kern,
        grid=(nb, ng, nq),
        in_specs=[pl.BlockSpec((tq, hpg * 2 * LANES), lambda b, g, i: (b * nq + i, g)),
                  pl.BlockSpec((seq, hpg * MLA_NOPE), lambda b, g, i: (b, g)),
                  pl.BlockSpec((seq, LANES), lambda b, g, i: (b, 0)),
                  pl.BlockSpec((seq, hpg * MLA_V), lambda b, g, i: (b, ng + g))],
        out_specs=pl.BlockSpec((tq, hpg * MLA_V), lambda b, g, i: (b * nq + i, g)),
        out_shape=jax.ShapeDtypeStruct((m, MLA_HEADS * MLA_V), BF16),
        scratch_shapes=[pltpu.VMEM((hpg, seq // tk, MLA_V, tk), BF16),
                        pltpu.VMEM((hpg, MLA_V, tq), F32)],
        compiler_params=_params(("parallel", "parallel", "arbitrary")),
        name="attention",
    )(q, kv, kpe, kv)


def _ffn_up_kernel(x_ref, nw_ref, wg_ref, wv_ref, cw_ref, cb_ref, o_ref, xn_ref, halo_ref,
                   *, tm, tiles_per_seq):
    i = pl.program_id(0)
    j = pl.program_id(1)

    @pl.when(j == 0)
    def _():
        xn_ref[...] = _rms(x_ref[...], nw_ref[...]).astype(BF16)

    @pl.when(i % tiles_per_seq == 0)
    def _():
        halo_ref[j] = jnp.zeros(halo_ref.shape[1:], F32)

    xn = xn_ref[...]
    g = _dot(xn, wg_ref[...].astype(BF16))
    val = _dot(xn, wv_ref[...].astype(BF16))
    gext = jnp.concatenate([halo_ref[j], g], axis=0)
    halo_ref[j] = g[tm - SUBLANES:tm, :]
    acc = cw_ref[0:1, :] * gext
    for k in range(1, FFN_CONV):
        acc = pltpu.roll(acc, 1, axis=0) + cw_ref[k:k + 1, :] * gext
    conv = acc[SUBLANES:, :] + cb_ref[...]
    o_ref[...] = (_gelu(conv) * val).astype(o_ref.dtype)


def ffn_up(h, nw, w_up_all, conv_w_all, conv_b_all, layer, seq):
    m, k = h.shape
    dff = w_up_all.shape[-1] // 2
    tm = _tile(seq, 1024)
    tn = _tile(dff, 512)
    nj = dff // tn
    kern = functools.partial(_ffn_up_kernel, tm=tm, tiles_per_seq=seq // tm)
    return pl.pallas_call(
        kern,
        grid=(m // tm, nj),
        in_specs=[pl.BlockSpec((tm, k), lambda i, j: (i, 0)),
                  pl.BlockSpec((1, k), lambda i, j: (0, 0)),
                  pl.BlockSpec((None, k, tn), lambda i, j: (layer, 0, j)),
                  pl.BlockSpec((None, k, tn), lambda i, j: (layer, 0, nj + j)),
                  pl.BlockSpec((None, FFN_CONV, tn), lambda i, j: (layer, 0, j)),
                  pl.BlockSpec((None, 1, tn), lambda i, j: (layer, 0, j))],
        out_specs=pl.BlockSpec((tm, tn), lambda i, j: (i, j)),
        out_shape=jax.ShapeDtypeStruct((m, dff), BF16),
        scratch_shapes=[pltpu.VMEM((tm, k), BF16),
                        pltpu.VMEM((nj, SUBLANES, tn), F32)],
        compiler_params=_params(("arbitrary", "arbitrary")),
        name="ffn_up",
    )(h, nw, w_up_all, w_up_all, conv_w_all, conv_b_all)


def _rmsnorm_kernel(x_ref, w_ref, o_ref):
    o_ref[...] = _rms(x_ref[...], w_ref[...])


def rmsnorm(x, w):
    m, k = x.shape
    tm = _tile(m, 512)
    return pl.pallas_call(
        _rmsnorm_kernel,
        grid=(m // tm,),
        in_specs=[pl.BlockSpec((tm, k), lambda i: (i, 0)),
                  pl.BlockSpec((1, k), lambda i: (0, 0))],
        out_specs=pl.BlockSpec((tm, k), lambda i: (i, 0)),
        out_shape=jax.ShapeDtypeStruct((m, k), F32),
        compiler_params=_params(("parallel",)),
        name="final_rmsnorm",
    )(x, w)


def _rot_half_cols(w):
    half = w.shape[-1] // 2
    return jnp.concatenate([-w[..., half:], w[..., :half]], axis=-1)


def _pad_lanes(v):
    pad = (-v.shape[-1]) % LANES
    return jnp.pad(v, [(0, 0)] * (v.ndim - 1) + [(0, pad)])


def _even_layer(h, seq, nw, j, w_in_main, w_in_t, ln_g, ln_b, ws, bs, conv_w, conv_b, dt_bias,
                a_log, d_skip, ssm_nw, ev_w_out):
    width = h.shape[1]
    proj, dt_raw = even_inproj(h, nw[None, :], w_in_main, w_in_t, j)
    ya = sgu(proj, ln_g.reshape(1, width), ln_b.reshape(1, width), ws, bs.T, width)
    head_of_col = jnp.arange(width) // SSM_HEAD_DIM
    e_mat = (jnp.arange(LANES)[:, None] == head_of_col[None, :]).astype(BF16)
    yb = ssd(proj, dt_raw, conv_w, conv_b[None, :], _pad_lanes(dt_bias[None, :]),
             _pad_lanes(a_log[None, :]), jnp.repeat(d_skip, SSM_HEAD_DIM)[None, :],
             ssm_nw[None, :], e_mat, seq, width)
    return matmul2_res(ya, yb, ev_w_out, j, h, 512)


def _odd_layer(h, seq, cs, nw, j, w_in, qn, kvn, w_uq, w_ukv, od_w_o):
    r = MLA_RANK
    qk = MLA_NOPE + MLA_ROPE
    w_kr = w_in[:, 2 * r:]
    w_in2 = jnp.concatenate([w_in, _rot_half_cols(w_kr)], axis=1).astype(BF16)
    wq = w_uq.reshape(r, MLA_HEADS, qk)
    wq_pe = wq[:, :, MLA_NOPE:]
    wq2 = jnp.concatenate([wq[:, :, :MLA_NOPE], wq_pe, _rot_half_cols(wq_pe)], axis=-1)
    wq2 = wq2.reshape(r, MLA_HEADS * 2 * LANES).astype(BF16)
    wkv = w_ukv.reshape(r, MLA_HEADS, MLA_NOPE + MLA_V)
    wkv2 = jnp.concatenate([wkv[:, :, :MLA_NOPE].reshape(r, MLA_HEADS * MLA_NOPE),
                            wkv[:, :, MLA_NOPE:].reshape(r, MLA_HEADS * MLA_V)], axis=1)
    q, kv, kpe = mla_proj(h, nw[None, :], w_in2, qn[None, :], kvn[None, :], wq2,
                          wkv2.astype(BF16), cs, qk ** -0.5 * LOG2E)
    o = attention(q, kv, kpe, seq)
    return matmul_res(o, od_w_o, j, h, 1024)


def kernel(x, positions, norm_mix, norm_ffn, norm_final, ev_w_in, ev_gm_ln_g, ev_gm_ln_b, ev_gm_ws, ev_gm_bs, ev_conv_w, ev_conv_b, ev_dt_bias, ev_a_log, ev_d_skip, ev_ssm_norm_w, ev_w_out, od_w_in, od_q_norm, od_kv_norm, od_w_uq, od_w_ukv, od_w_o, ff_w_up, ff_conv_w, ff_conv_b, ff_w_down):
    bsz, seq, d = x.shape
    m = bsz * seq
    depth = norm_mix.shape[0]

    inv_freq = ROPE_THETA ** (-jnp.arange(0, MLA_ROPE, 2, dtype=F32) / MLA_ROPE)
    freq_row = jnp.tile(inv_freq, 4)[None, :]
    pos_b = jnp.broadcast_to(positions.astype(F32).reshape(m, 1), (m, LANES))
    cs = rope_table(pos_b, freq_row)
    ff_conv_b3 = ff_conv_b[:, None, :]
    w_down_bf16 = cast_bf16(ff_w_down)
    od_w_o = cast_bf16(od_w_o)
    w_in_t = jnp.swapaxes(ev_w_in, 1, 2)
    n_main = 3 * d + (d + 2 * SSM_GROUPS * SSM_STATE)
    w_in_main = transpose_cast(w_in_t, n_main)

    h = x.reshape(m, d)
    for layer in range(depth):
        j = layer // 2
        if layer % 2 == 0:
            h = _even_layer(h, seq, norm_mix[layer], j, w_in_main, w_in_t, ev_gm_ln_g[j],
                            ev_gm_ln_b[j],
                            ev_gm_ws[j], ev_gm_bs[j], ev_conv_w[j], ev_conv_b[j], ev_dt_bias[j],
                            ev_a_log[j], ev_d_skip[j], ev_ssm_norm_w[j], ev_w_out)
        else:
            h = _odd_layer(h, seq, cs, norm_mix[layer], j, od_w_in[j], od_q_norm[j],
                           od_kv_norm[j], od_w_uq[j], od_w_ukv[j], od_w_o)
        act = ffn_up(h, norm_ffn[layer][None, :], ff_w_up, ff_conv_w, ff_conv_b3, layer, seq)
        h = matmul_res(act, w_down_bf16, layer, h, 512)
    return rmsnorm(h, norm_final[None, :]).reshape(bsz, seq, d)
```

```python
import functools

import jax
import jax.numpy as jnp
from jax import lax
from jax.experimental import pallas as pl
from jax.experimental.pallas import tpu as pltpu

F32 = jnp.float32
BF16 = jnp.bfloat16

EPS = 1e-6
LANES = 128
SUBLANES = 8
HALO = 16
VMEM_LIMIT = 56 * 1024 * 1024

CHUNK = 64
GM_BLOCK = 128
GM_GROUPS = 8
SSM_HEAD_DIM = 64
SSM_GROUPS = 4
SSM_STATE = 128
SSM_CONV = 4
MLA_HEADS = 16
MLA_NOPE = 128
MLA_ROPE = 64
MLA_V = 128
MLA_RANK = 512
ATTN_HEADS_PER_STEP = 8
ROPE_THETA = 10000.0
LOG2E = 1.4426950408889634
FFN_CONV = 3


def _params(sem):
    return pltpu.CompilerParams(dimension_semantics=sem, vmem_limit_bytes=VMEM_LIMIT)


def _dot(a, b):
    return jnp.dot(a, b, preferred_element_type=F32)


def _dot_nt(a, b):
    return lax.dot_general(a, b, (((1,), (1,)), ((), ())), preferred_element_type=F32)


def _dot_tn(a, b):
    return lax.dot_general(a, b, (((0,), (0,)), ((), ())), preferred_element_type=F32)


def _rms(x, w):
    ms = jnp.mean(x * x, axis=-1, keepdims=True)
    return x * lax.rsqrt(ms + EPS) * w


def _gelu(x):
    return 0.5 * x * (1.0 + jnp.tanh(0.7978845608028654 * (x + 0.044715 * (x * x * x))))


def _silu(x):
    hx = 0.5 * x
    return hx + hx * jnp.tanh(hx)


def _split3(x):
    hi = x.astype(BF16)
    r1 = x - hi.astype(F32)
    mid = r1.astype(BF16)
    lo = (r1 - mid.astype(F32)).astype(BF16)
    return hi, mid, lo


def _expand01(x, sel):
    hi, mid, lo = _split3(x)
    return _dot(hi, sel) + _dot(mid, sel) + _dot(lo, sel)


def _tile(n, pref):
    if n <= pref:
        return n
    t = pref
    while t >= LANES:
        if n % t == 0:
            return t
        t -= LANES
    return n


def _rope_table_kernel(pos_ref, freq_ref, o_ref):
    ang = pos_ref[...] * freq_ref[...]
    lane = lax.broadcasted_iota(jnp.int32, ang.shape, 1)
    o_ref[...] = jnp.where(lane < MLA_ROPE, jnp.cos(ang), jnp.sin(ang))


def rope_table(pos_b, freq_row):
    m = pos_b.shape[0]
    tm = _tile(m, 1024)
    return pl.pallas_call(
        _rope_table_kernel,
        grid=(m // tm,),
        in_specs=[pl.BlockSpec((tm, LANES), lambda i: (i, 0)),
                  pl.BlockSpec((1, LANES), lambda i: (0, 0))],
        out_specs=pl.BlockSpec((tm, LANES), lambda i: (i, 0)),
        out_shape=jax.ShapeDtypeStruct((m, LANES), F32),
        compiler_params=_params(("parallel",)),
        name="rope_table",
    )(pos_b, freq_row)


def _transpose_cast_kernel(x_ref, o_ref):
    o_ref[...] = x_ref[...].T.astype(o_ref.dtype)


def transpose_cast(wt_all, n_main, gd):
    nl, _, k = wt_all.shape
    assert n_main % gd == 0

    def src_block(j):
        g = j // 2
        return jnp.where(j < 2 * GM_GROUPS, jnp.where(j % 2 == 0, g, GM_GROUPS + g), j)

    return pl.pallas_call(
        _transpose_cast_kernel,
        grid=(nl, n_main // gd),
        in_specs=[pl.BlockSpec((None, gd, k), lambda l, j: (l, src_block(j), 0))],
        out_specs=pl.BlockSpec((None, k, gd), lambda l, j: (l, 0, j)),
        out_shape=jax.ShapeDtypeStruct((nl, k, n_main), BF16),
        compiler_params=_params(("parallel", "parallel")),
        name="transpose_cast",
    )(wt_all)


def _even_inproj_kernel(x_ref, nw_ref, w_ref, wdt_ref, ws_ref, lg_ref, lb_ref, bs_ref, cw_ref,
                        cb_ref, ya_ref, zg_ref, xc_ref, dt_ref, xn_ref, halo_ref, g_ref,
                        *, tm, sub, nz, tiles_per_seq):
    i = pl.program_id(0)
    j = pl.program_id(1)

    @pl.when(j == 0)
    def _():
        xn = _rms(x_ref[...], nw_ref[...]).astype(BF16)
        xn_ref[...] = xn
        n_dt, k = wdt_ref.shape
        wdt = jnp.concatenate([wdt_ref[...], jnp.zeros((LANES - n_dt, k), F32)], axis=0)
        dt_ref[...] = _dot_nt(xn, wdt.astype(BF16))

    @pl.when(j < GM_GROUPS)
    def _():
        gd = ya_ref.shape[1]
        row = lax.broadcasted_iota(jnp.int32, (GM_BLOCK, GM_BLOCK), 0) // CHUNK
        col = lax.broadcasted_iota(jnp.int32, (GM_BLOCK, GM_BLOCK), 1) // CHUNK
        wsg = jnp.where(row >= col, ws_ref[...], 0.0).astype(BF16)
        ln_g = lg_ref[...]
        ln_b = lb_ref[...]
        bsg = bs_ref[...]
        w = w_ref[...]
        for r0 in range(0, tm, sub):
            g_ref[SUBLANES + r0:SUBLANES + r0 + sub, :] = _dot(xn_ref[r0:r0 + sub, :], w)
        for r0 in range(0, tm, GM_BLOCK):
            rows = slice(SUBLANES + r0, SUBLANES + r0 + GM_BLOCK)
            v = _gelu(g_ref[rows, gd:])
            mu = jnp.mean(v, axis=-1, keepdims=True)
            d = v - mu
            var = jnp.mean(d * d, axis=-1, keepdims=True)
            vn = (d * lax.rsqrt(var + EPS) * ln_g + ln_b).astype(BF16)
            gate = _dot(wsg, vn) + bsg
            ya_ref[r0:r0 + GM_BLOCK, :] = (_gelu(g_ref[rows, :gd]) * gate).astype(ya_ref.dtype)

    @pl.when((j >= GM_GROUPS) & (j < GM_GROUPS + nz))
    def _():
        zg_ref[...] = _silu(_dot(xn_ref[...], w_ref[...])).astype(zg_ref.dtype)

    @pl.when(j >= GM_GROUPS + nz)
    def _():
        jx = j - (GM_GROUPS + nz)

        @pl.when(i % tiles_per_seq == 0)
        def _():
            halo_ref[jx] = jnp.zeros(halo_ref.shape[1:], F32)

        w = w_ref[...]
        g_ref[0:SUBLANES, :] = halo_ref[jx]
        for r0 in range(0, tm, sub):
            g_ref[SUBLANES + r0:SUBLANES + r0 + sub, :] = _dot(xn_ref[r0:r0 + sub, :], w)
        halo_ref[jx] = g_ref[tm:tm + SUBLANES, :]
        for r0 in range(0, tm, sub):
            gext = g_ref[r0:r0 + SUBLANES + sub, :]
            acc = cw_ref[0:1, :] * gext
            for k in range(1, SSM_CONV):
                acc = pltpu.roll(acc, 1, axis=0) + cw_ref[k:k + 1, :] * gext
            xc_ref[r0:r0 + sub, :] = _silu(acc[SUBLANES:, :] + cb_ref[...]).astype(xc_ref.dtype)


def even_inproj(h, nw, w_main, wt_all, layer, ws, ln_g, ln_b, bs, conv_w, conv_b, seq):
    m, k = h.shape
    width = k
    gd = width // GM_GROUPS
    tn = 2 * gd
    conv_dim = conv_w.shape[1]
    n_main = w_main.shape[-1]
    n_dt = wt_all.shape[1] - n_main
    assert n_main == 3 * width + conv_dim and width % tn == 0 and conv_dim % tn == 0
    assert 0 < n_dt < LANES and n_dt % SUBLANES == 0 and n_main % n_dt == 0
    nz = width // tn
    nx = conv_dim // tn
    tm = _tile(seq, 1024)
    assert tm % GM_BLOCK == 0
    last_g = GM_GROUPS - 1
    grp = lambda i, j: (layer, jnp.minimum(j, last_g), 0, 0)
    xcol = lambda i, j: (0, jnp.clip(j - (GM_GROUPS + nz), 0, nx - 1))
    sub = tm // 4 if tm % (4 * GM_BLOCK) == 0 else tm
    kern = functools.partial(_even_inproj_kernel, tm=tm, sub=sub, nz=nz,
                             tiles_per_seq=seq // tm)
    return pl.pallas_call(
        kern,
        grid=(m // tm, GM_GROUPS + nz + nx),
        in_specs=[pl.BlockSpec((tm, k), lambda i, j: (i, 0)),
                  pl.BlockSpec((1, k), lambda i, j: (0, 0)),
                  pl.BlockSpec((None, k, tn), lambda i, j: (layer, 0, j)),
                  pl.BlockSpec((None, n_dt, k), lambda i, j: (layer, n_main // n_dt, 0)),
                  pl.BlockSpec((None, None, GM_BLOCK, GM_BLOCK), grp),
                  pl.BlockSpec((None, None, 1, gd), grp),
                  pl.BlockSpec((None, None, 1, gd), grp),
                  pl.BlockSpec((None, None, GM_BLOCK, 1), grp),
                  pl.BlockSpec((SSM_CONV, tn), xcol),
                  pl.BlockSpec((1, tn), xcol)],
        out_specs=[pl.BlockSpec((tm, gd), lambda i, j: (i, jnp.minimum(j, last_g))),
                   pl.BlockSpec((tm, tn), lambda i, j: (i, jnp.clip(j - GM_GROUPS, 0, nz - 1))),
                   pl.BlockSpec((tm, tn), lambda i, j: (i, xcol(i, j)[1])),
                   pl.BlockSpec((tm, LANES), lambda i, j: (i, 0))],
        out_shape=[jax.ShapeDtypeStruct((m, width), BF16),
                   jax.ShapeDtypeStruct((m, width), BF16),
                   jax.ShapeDtypeStruct((m, conv_dim), BF16),
                   jax.ShapeDtypeStruct((m, LANES), F32)],
        scratch_shapes=[pltpu.VMEM((tm, k), BF16),
                        pltpu.VMEM((nx, SUBLANES, tn), F32),
                        pltpu.VMEM((SUBLANES + tm, tn), F32)],
        compiler_params=_params(("arbitrary", "arbitrary")),
        name="even_inproj",
    )(h, nw, w_main, wt_all, ws, ln_g, ln_b, bs, conv_w, conv_b)


def _ssd_kernel(zg_ref, xc_ref, dt_ref, dtb_ref, alog_ref, dsk_ref, nw_ref, e_ref, o_ref,
                cum_ref, cumx_ref, eac_ref, xdd_ref, xtop_ref, xbot_ref, y_ref, h_ref,
                *, t_rows, d_inner):
    L = CHUNK
    P = SSM_HEAD_DIM
    N = SSM_STATE
    G = SSM_GROUPS
    gw = d_inner // G
    pairs_per_group = gw // LANES
    cchunk = 512

    @pl.when(pl.program_id(1) == 0)
    def _():
        h_ref[...] = jnp.zeros_like(h_ref)

    nchunk = t_rows // L
    dtr = dt_ref[...] + dtb_ref[...]
    dt = jnp.maximum(dtr, 0.0) + jnp.log1p(jnp.exp(-jnp.abs(dtr)))
    a = -jnp.exp(alog_ref[...])
    rt = lax.broadcasted_iota(jnp.int32, (t_rows, t_rows), 0)
    ct = lax.broadcasted_iota(jnp.int32, (t_rows, t_rows), 1)
    tril = jnp.where((rt >= ct) & (rt // L == ct // L), 1.0, 0.0).astype(BF16)
    d_hi, d_mid, d_lo = _split3(dt * a)
    cum = _dot(tril, d_hi) + _dot(tril, d_mid) + _dot(tril, d_lo)
    cum_ref[...] = cum

    for c0 in range(0, d_inner, cchunk):
        cs_ = slice(c0, c0 + cchunk)
        e_c = e_ref[:, cs_]
        cum_x = _expand01(cum, e_c)
        xd = xc_ref[:, cs_].astype(F32) * _expand01(dt, e_c)
        last = jnp.concatenate(
            [jnp.broadcast_to(cum_x[c * L + L - 1:c * L + L, :], (L, cchunk))
             for c in range(nchunk)], axis=0)
        cumx_ref[:, cs_] = cum_x
        eac_ref[:, cs_] = jnp.exp(cum_x)
        xdd_ref[:, cs_] = (xd * jnp.exp(last - cum_x)).astype(BF16)
        first = (lax.broadcasted_iota(jnp.int32, (t_rows, cchunk), 1) % LANES) < P
        xtop_ref[:, cs_] = jnp.where(first, xd, 0.0).astype(BF16)
        xbot_ref[:, cs_] = jnp.where(first, 0.0, xd).astype(BF16)

    ri2 = lax.broadcasted_iota(jnp.int32, (L, LANES), 0)
    ci2 = lax.broadcasted_iota(jnp.int32, (L, LANES), 1)
    lower2 = ri2 >= jnp.where(ci2 < P, ci2, ci2 - P)

    for c in range(nchunk):
        rows = slice(c * L, (c + 1) * L)
        cum_t = cum_ref[rows, :].T
        for g in range(G):
            gc = slice(g * gw, (g + 1) * gw)
            b_g = xc_ref[rows, d_inner + g * N:d_inner + (g + 1) * N]
            c_g = xc_ref[rows, d_inner + (G + g) * N:d_inner + (G + g + 1) * N]
            cb = _dot_nt(c_g, b_g)
            cb2 = jnp.concatenate([cb, cb], axis=1)
            eac_x = eac_ref[rows, gc]
            h_g = h_ref[g]
            y_off = _dot(c_g, h_g.astype(BF16)) * eac_x
            st = _dot_tn(b_g, xdd_ref[rows, gc])
            h_ref[g] = h_g * eac_x[L - 1:L, :] + st
            parts = []
            for q in range(pairs_per_group):
                hd = (g * pairs_per_group + q) * 2
                lanes = slice(g * gw + q * LANES, g * gw + (q + 1) * LANES)
                col_b = cumx_ref[rows, lanes]
                row_b = jnp.concatenate([cum_t[hd:hd + 1, :], cum_t[hd + 1:hd + 2, :]], axis=1)
                dec = jnp.exp(jnp.where(lower2, col_b - row_b, -jnp.inf))
                mm = (cb2 * dec).astype(BF16)
                rhs = jnp.concatenate([xtop_ref[rows, lanes], xbot_ref[rows, lanes]], axis=0)
                parts.append(_dot(mm, rhs))
            y_ref[rows, gc] = jnp.concatenate(parts, axis=1) + y_off

    for g in range(G):
        gc = slice(g * gw, (g + 1) * gw)
        y = y_ref[:, gc] + dsk_ref[:, gc] * xc_ref[:, gc].astype(F32)
        y = y * zg_ref[:, gc].astype(F32)
        o_ref[:, gc] = _rms(y, nw_ref[:, gc]).astype(o_ref.dtype)


def ssd(zg, xc, dt_raw, dt_bias, a_log, d_skip_x, norm_w, e_mat, seq):
    m, d_inner = zg.shape
    conv_dim = xc.shape[1]
    assert 2 * SSM_HEAD_DIM == LANES and conv_dim == d_inner + 2 * SSM_GROUPS * SSM_STATE
    t = _tile(seq, 256)
    spb = seq // t
    nb = m // seq
    kern = functools.partial(_ssd_kernel, t_rows=t, d_inner=d_inner)
    row = lambda b, s: (b * spb + s, 0)
    const = lambda b, s: (0, 0)
    return pl.pallas_call(
        kern,
        grid=(nb, spb),
        in_specs=[pl.BlockSpec((t, d_inner), row),
                  pl.BlockSpec((t, conv_dim), row),
                  pl.BlockSpec((t, LANES), row),
                  pl.BlockSpec((1, LANES), const),
                  pl.BlockSpec((1, LANES), const),
                  pl.BlockSpec((1, d_inner), const),
                  pl.BlockSpec((1, d_inner), const),
                  pl.BlockSpec((LANES, d_inner), const)],
        out_specs=pl.BlockSpec((t, d_inner), row),
        out_shape=jax.ShapeDtypeStruct((m, d_inner), BF16),
        scratch_shapes=[pltpu.VMEM((t, LANES), F32),
                        pltpu.VMEM((t, d_inner), F32),
                        pltpu.VMEM((t, d_inner), F32),
                        pltpu.VMEM((t, d_inner), BF16),
                        pltpu.VMEM((t, d_inner), BF16),
                        pltpu.VMEM((t, d_inner), BF16),
                        pltpu.VMEM((t, d_inner), F32),
                        pltpu.VMEM((SSM_GROUPS, SSM_STATE, d_inner // SSM_GROUPS), F32)],
        compiler_params=_params(("arbitrary", "arbitrary")),
        name="ssd",
    )(zg, xc, dt_raw, dt_bias, a_log, d_skip_x, norm_w, e_mat)


def _cast_kernel(x_ref, o_ref):
    o_ref[...] = x_ref[...].astype(o_ref.dtype)


def cast_bf16(w_all):
    nl, k, n = w_all.shape
    tk = _tile(k, 512)
    tn = _tile(n, 2048)
    return pl.pallas_call(
        _cast_kernel,
        grid=(nl, k // tk, n // tn),
        in_specs=[pl.BlockSpec((None, tk, tn), lambda l, i, j: (l, i, j))],
        out_specs=pl.BlockSpec((None, tk, tn), lambda l, i, j: (l, i, j)),
        out_shape=jax.ShapeDtypeStruct((nl, k, n), BF16),
        compiler_params=_params(("parallel", "parallel", "parallel")),
        name="cast_bf16",
    )(w_all)


def _mm_res_kernel(x_ref, w_ref, r_ref, o_ref):
    o_ref[...] = r_ref[...] + _dot(x_ref[...], w_ref[...].astype(BF16))


def matmul_res(x, w_all, layer, res, tn_pref):
    m, k = x.shape
    n = w_all.shape[-1]
    tm = _tile(m, 1024)
    tn = _tile(n, tn_pref)
    return pl.pallas_call(
        _mm_res_kernel,
        grid=(m // tm, n // tn),
        in_specs=[pl.BlockSpec((tm, k), lambda i, j: (i, 0)),
                  pl.BlockSpec((None, k, tn), lambda i, j: (layer, 0, j)),
                  pl.BlockSpec((tm, tn), lambda i, j: (i, j))],
        out_specs=pl.BlockSpec((tm, tn), lambda i, j: (i, j)),
        out_shape=jax.ShapeDtypeStruct((m, n), F32),
        compiler_params=_params(("parallel", "parallel")),
        name="matmul_res",
    )(x, w_all, res)


def _mm2_res_kernel(xa_ref, xb_ref, wa_ref, wb_ref, r_ref, o_ref):
    o_ref[...] = (r_ref[...] + _dot(xa_ref[...], wa_ref[...].astype(BF16))
                  + _dot(xb_ref[...], wb_ref[...].astype(BF16)))


def matmul2_res(xa, xb, w_all, layer, res, tn_pref):
    m, k = xa.shape
    n = w_all.shape[-1]
    tm = _tile(m, 1024)
    tn = _tile(n, tn_pref)
    return pl.pallas_call(
        _mm2_res_kernel,
        grid=(m // tm, n // tn),
        in_specs=[pl.BlockSpec((tm, k), lambda i, j: (i, 0)),
                  pl.BlockSpec((tm, k), lambda i, j: (i, 0)),
                  pl.BlockSpec((None, k, tn), lambda i, j: (layer, 0, j)),
                  pl.BlockSpec((None, k, tn), lambda i, j: (layer, 1, j)),
                  pl.BlockSpec((tm, tn), lambda i, j: (i, j))],
        out_specs=pl.BlockSpec((tm, tn), lambda i, j: (i, j)),
        out_shape=jax.ShapeDtypeStruct((m, n), F32),
        compiler_params=_params(("parallel", "parallel")),
        name="matmul2_res",
    )(xa, xb, w_all, w_all, res)


def _rope_lanes(c2, cs):
    t = c2 * cs
    return t + pltpu.roll(t, MLA_ROPE, axis=1)


def _mla_proj_kernel(x_ref, nw_ref, win_ref, qn_ref, kvn_ref, wq_ref, wkv_ref, cs_ref,
                     q_ref, kv_ref, kpe_ref, *, scale, group):
    xn = _rms(x_ref[...], nw_ref[...]).astype(BF16)
    p = _dot(xn, win_ref[...])
    r = MLA_RANK
    cq = _rms(p[:, 0:r], qn_ref[...]).astype(BF16)
    ckv = _rms(p[:, r:2 * r], kvn_ref[...]).astype(BF16)
    cs = cs_ref[...]
    kr = _rope_lanes(p[:, 2 * r:2 * r + LANES], cs)
    lane = lax.broadcasted_iota(jnp.int32, kr.shape, 1)
    kpe_ref[...] = jnp.where(lane < MLA_ROPE, kr, 0.0).astype(BF16)

    hw = 2 * LANES
    for c0 in range(0, kv_ref.shape[1], group * hw):
        cols = slice(c0, c0 + group * hw)
        kv_ref[:, cols] = _dot(ckv, wkv_ref[:, cols]).astype(BF16)
        q = _dot(cq, wq_ref[:, cols])
        for hh in range(group):
            lo = c0 + hh * hw
            q_ref[:, lo:lo + LANES] = (q[:, hh * hw:hh * hw + LANES] * scale).astype(BF16)
            pe = _rope_lanes(q[:, hh * hw + LANES:(hh + 1) * hw], cs)
            q_ref[:, lo + LANES:lo + hw] = (pe * scale).astype(BF16)


def mla_proj(h, nw, w_in, qn, kvn, wq, wkv, cs, scale):
    m, k = h.shape
    nq = wq.shape[1]
    assert wkv.shape[1] == nq
    tm = _tile(m, 512)
    row = lambda i: (i, 0)
    const = lambda i: (0, 0)
    resident = pl.Buffered(1)
    return pl.pallas_call(
        functools.partial(_mla_proj_kernel, scale=scale, group=4),
        grid=(m // tm,),
        in_specs=[pl.BlockSpec((tm, k), row),
                  pl.BlockSpec((1, k), const),
                  pl.BlockSpec(w_in.shape, const, pipeline_mode=resident),
                  pl.BlockSpec((1, MLA_RANK), const),
                  pl.BlockSpec((1, MLA_RANK), const),
                  pl.BlockSpec(wq.shape, const, pipeline_mode=resident),
                  pl.BlockSpec(wkv.shape, const, pipeline_mode=resident),
                  pl.BlockSpec((tm, LANES), row)],
        out_specs=[pl.BlockSpec((tm, nq), row),
                   pl.BlockSpec((tm, nq), row),
                   pl.BlockSpec((tm, LANES), row)],
        out_shape=[jax.ShapeDtypeStruct((m, nq), BF16),
                   jax.ShapeDtypeStruct((m, nq), BF16),
                   jax.ShapeDtypeStruct((m, LANES), BF16)],
        compiler_params=_params(("parallel",)),
        name="mla_proj",
    )(h, nw, w_in, qn, kvn, wq, wkv, cs)


def _attn_kernel(q_ref, k_ref, kpe_ref, v_ref, o_ref, vt_ref, acc_ref, *, tq, tk, seq, hpg):
    i = pl.program_id(2)
    qw = 2 * LANES

    @pl.when(i == 0)
    def _():
        for h in range(hpg):
            for c in range(seq // tk):
                blk = v_ref[c * tk:(c + 1) * tk, h * MLA_V:(h + 1) * MLA_V].astype(F32)
                vt_ref[h, c] = blk.T.astype(BF16)

    acc_ref[...] = jnp.zeros(acc_ref.shape, F32)

    def scores(kb):
        ks = pl.multiple_of(kb * tk, tk)
        kpe = kpe_ref[pl.ds(ks, tk), :]
        out = []
        for h in range(hpg):
            kk = jnp.concatenate([k_ref[pl.ds(ks, tk), h * MLA_NOPE:(h + 1) * MLA_NOPE], kpe],
                                 axis=1)
            out.append(_dot_nt(kk, q_ref[:, h * qw:(h + 1) * qw]))
        return tuple(out)

    def update(kb, ms, ls, ss):
        new_ms, new_ls, alphas, ps = [], [], [], []
        for h in range(hpg):
            m_new = jnp.maximum(ms[h], jnp.max(ss[h], axis=0, keepdims=True))
            alpha = jnp.exp2(ms[h] - m_new)
            p = jnp.exp2(ss[h] - m_new)
            new_ls.append(alpha * ls[h] + jnp.sum(p, axis=0, keepdims=True))
            new_ms.append(m_new)
            alphas.append(alpha)
            ps.append(p.astype(BF16))
        pvs = [_dot(vt_ref[h, kb], ps[h]) for h in range(hpg)]
        for h in range(hpg):
            acc_ref[h] = alphas[h] * acc_ref[h] + pvs[h]
        return tuple(new_ms), tuple(new_ls)

    def body(kb, carry):
        ms, ls = carry
        return update(kb, ms, ls, scores(kb))

    init = (tuple(jnp.full((1, tq), -jnp.inf, F32) for _ in range(hpg)),
            tuple(jnp.zeros((1, tq), F32) for _ in range(hpg)))
    ndiag = (i * tq) // tk
    ms, ls = lax.fori_loop(0, ndiag, body, init)
    kc = lax.broadcasted_iota(jnp.int32, (tk, tq), 0) // CHUNK
    qc = lax.broadcasted_iota(jnp.int32, (tk, tq), 1) // CHUNK
    ss = tuple(jnp.where(kc <= qc, s, -jnp.inf) for s in scores(ndiag))
    _, ls = update(ndiag, ms, ls, ss)

    for h in range(hpg):
        o = acc_ref[h] * (1.0 / ls[h])
        o_ref[:, h * MLA_V:(h + 1) * MLA_V] = o.T.astype(o_ref.dtype)


def attention(q, kv, kpe, seq):
    m = q.shape[0]
    nb = m // seq
    tq = _tile(seq, 256)
    tk = tq
    nq = seq // tq
    hpg = ATTN_HEADS_PER_STEP
    ng = MLA_HEADS // hpg
    kern = functools.partial(_attn_kernel, tq=tq, tk=tk, seq=seq, hpg=hpg)
    return pl.pallas_call(
        kern,
        grid=(nb, ng, nq),
        in_specs=[pl.BlockSpec((tq, hpg * 2 * LANES), lambda b, g, i: (b * nq + i, g)),
                  pl.BlockSpec((seq, hpg * MLA_NOPE), lambda b, g, i: (b, g)),
                  pl.BlockSpec((seq, LANES), lambda b, g, i: (b, 0)),
                  pl.BlockSpec((seq, hpg * MLA_V), lambda b, g, i: (b, ng + g))],
        out_specs=pl.BlockSpec((tq, hpg * MLA_V), lambda b, g, i: (b * nq + i, g)),
        out_shape=jax.ShapeDtypeStruct((m, MLA_HEADS * MLA_V), BF16),
        scratch_shapes=[pltpu.VMEM((hpg, seq // tk, MLA_V, tk), BF16),
                        pltpu.VMEM((hpg, MLA_V, tq), F32)],
        compiler_params=_params(("parallel", "parallel", "arbitrary")),
        name="attention",
    )(q, kv, kpe, kv)


def _ffn_up_kernel(x_ref, nw_ref, wg_ref, wv_ref, cw_ref, cb_ref, o_ref, xn_ref, halo_ref,
                   *, tm, tiles_per_seq):
    i = pl.program_id(0)
    j = pl.program_id(1)

    @pl.when(j == 0)
    def _():
        xn_ref[...] = _rms(x_ref[...], nw_ref[...]).astype(BF16)

    @pl.when(i % tiles_per_seq == 0)
    def _():
        halo_ref[j] = jnp.zeros(halo_ref.shape[1:], F32)

    xn = xn_ref[...]
    g = _dot(xn, wg_ref[...].astype(BF16))
    val = _dot(xn, wv_ref[...].astype(BF16))
    gext = jnp.concatenate([halo_ref[j], g], axis=0)
    halo_ref[j] = g[tm - SUBLANES:tm, :]
    acc = cw_ref[0:1, :] * gext
    for k in range(1, FFN_CONV):
        acc = pltpu.roll(acc, 1, axis=0) + cw_ref[k:k + 1, :] * gext
    conv = acc[SUBLANES:, :] + cb_ref[...]
    o_ref[...] = (_gelu(conv) * val).astype(o_ref.dtype)


def ffn_up(h, nw, w_up_all, conv_w_all, conv_b_all, layer, seq):
    m, k = h.shape
    dff = w_up_all.shape[-1] // 2
    tm = _tile(seq, 1024)
    tn = _tile(dff, 512)
    nj = dff // tn
    kern = functools.partial(_ffn_up_kernel, tm=tm, tiles_per_seq=seq // tm)
    return pl.pallas_call(
        kern,
        grid=(m // tm, nj),
        in_specs=[pl.BlockSpec((tm, k), lambda i, j: (i, 0)),
                  pl.BlockSpec((1, k), lambda i, j: (0, 0)),
                  pl.BlockSpec((None, k, tn), lambda i, j: (layer, 0, j)),
                  pl.BlockSpec((None, k, tn), lambda i, j: (layer, 0, nj + j)),
                  pl.BlockSpec((None, FFN_CONV, tn), lambda i, j: (layer, 0, j)),
                  pl.BlockSpec((None, 1, tn), lambda i, j: (layer, 0, j))],
        out_specs=pl.BlockSpec((tm, tn), lambda i, j: (i, j)),
        out_shape=jax.ShapeDtypeStruct((m, dff), BF16),
        scratch_shapes=[pltpu.VMEM((tm, k), BF16),
                        pltpu.VMEM((nj, SUBLANES, tn), F32)],
        compiler_params=_params(("arbitrary", "arbitrary")),
        name="ffn_up",
    )(h, nw, w_up_all, w_up_all, conv_w_all, conv_b_all)


def _rmsnorm_kernel(x_ref, w_ref, o_ref):
    o_ref[...] = _rms(x_ref[...], w_ref[...])


def rmsnorm(x, w):
    m, k = x.shape
    tm = _tile(m, 512)
    return pl.pallas_call(
        _rmsnorm_kernel,
        grid=(m // tm,),
        in_specs=[pl.BlockSpec((tm, k), lambda i: (i, 0)),
                  pl.BlockSpec((1, k), lambda i: (0, 0))],
        out_specs=pl.BlockSpec((tm, k), lambda i: (i, 0)),
        out_shape=jax.ShapeDtypeStruct((m, k), F32),
        compiler_params=_params(("parallel",)),
        name="final_rmsnorm",
    )(x, w)


def _rot_half_cols(w):
    half = w.shape[-1] // 2
    return jnp.concatenate([-w[..., half:], w[..., :half]], axis=-1)


def _pad_lanes(v):
    pad = (-v.shape[-1]) % LANES
    return jnp.pad(v, [(0, 0)] * (v.ndim - 1) + [(0, pad)])


def _even_layer(h, seq, nw, j, w_in_main, w_in_t, ln_g4, ln_b4, ws4, bs4, conv_w, conv_b, dt_bias,
                a_log, d_skip, ssm_nw, ev_w_out):
    width = h.shape[1]
    ya, zg, xc, dt_raw = even_inproj(h, nw[None, :], w_in_main, w_in_t, j, ws4, ln_g4, ln_b4,
                                     bs4, conv_w, conv_b[None, :], seq)
    head_of_col = jnp.arange(width) // SSM_HEAD_DIM
    e_mat = (jnp.arange(LANES)[:, None] == head_of_col[None, :]).astype(BF16)
    yb = ssd(zg, xc, dt_raw, _pad_lanes(dt_bias[None, :]), _pad_lanes(a_log[None, :]),
             jnp.repeat(d_skip, SSM_HEAD_DIM)[None, :], ssm_nw[None, :], e_mat, seq)
    return matmul2_res(ya, yb, ev_w_out, j, h, 512)


def _odd_layer(h, seq, cs, nw, j, w_in, qn, kvn, w_uq, w_ukv, od_w_o):
    r = MLA_RANK
    qk = MLA_NOPE + MLA_ROPE
    w_kr = w_in[:, 2 * r:]
    w_in2 = jnp.concatenate([w_in, _rot_half_cols(w_kr)], axis=1).astype(BF16)
    wq = w_uq.reshape(r, MLA_HEADS, qk)
    wq_pe = wq[:, :, MLA_NOPE:]
    wq2 = jnp.concatenate([wq[:, :, :MLA_NOPE], wq_pe, _rot_half_cols(wq_pe)], axis=-1)
    wq2 = wq2.reshape(r, MLA_HEADS * 2 * LANES).astype(BF16)
    wkv = w_ukv.reshape(r, MLA_HEADS, MLA_NOPE + MLA_V)
    wkv2 = jnp.concatenate([wkv[:, :, :MLA_NOPE].reshape(r, MLA_HEADS * MLA_NOPE),
                            wkv[:, :, MLA_NOPE:].reshape(r, MLA_HEADS * MLA_V)], axis=1)
    q, kv, kpe = mla_proj(h, nw[None, :], w_in2, qn[None, :], kvn[None, :], wq2,
                          wkv2.astype(BF16), cs, qk ** -0.5 * LOG2E)
    o = attention(q, kv, kpe, seq)
    return matmul_res(o, od_w_o, j, h, 1024)


def kernel(x, positions, norm_mix, norm_ffn, norm_final, ev_w_in, ev_gm_ln_g, ev_gm_ln_b, ev_gm_ws, ev_gm_bs, ev_conv_w, ev_conv_b, ev_dt_bias, ev_a_log, ev_d_skip, ev_ssm_norm_w, ev_w_out, od_w_in, od_q_norm, od_kv_norm, od_w_uq, od_w_ukv, od_w_o, ff_w_up, ff_conv_w, ff_conv_b, ff_w_down):
    bsz, seq, d = x.shape
    m = bsz * seq
    depth = norm_mix.shape[0]

    inv_freq = ROPE_THETA ** (-jnp.arange(0, MLA_ROPE, 2, dtype=F32) / MLA_ROPE)
    freq_row = jnp.tile(inv_freq, 4)[None, :]
    pos_b = jnp.broadcast_to(positions.astype(F32).reshape(m, 1), (m, LANES))
    cs = rope_table(pos_b, freq_row)
    ff_conv_b3 = ff_conv_b[:, None, :]
    w_down_bf16 = cast_bf16(ff_w_down)
    od_w_o = cast_bf16(od_w_o)
    w_in_t = jnp.swapaxes(ev_w_in, 1, 2)
    n_main = 3 * d + (d + 2 * SSM_GROUPS * SSM_STATE)
    w_in_main = transpose_cast(w_in_t, n_main, d // GM_GROUPS)
    ln_g4 = ev_gm_ln_g[:, :, None, :]
    ln_b4 = ev_gm_ln_b[:, :, None, :]
    bs4 = ev_gm_bs[:, :, :, None]

    h = x.reshape(m, d)
    for layer in range(depth):
        j = layer // 2
        if layer % 2 == 0:
            h = _even_layer(h, seq, norm_mix[layer], j, w_in_main, w_in_t, ln_g4, ln_b4,
                            ev_gm_ws, bs4, ev_conv_w[j], ev_conv_b[j], ev_dt_bias[j],
                            ev_a_log[j], ev_d_skip[j], ev_ssm_norm_w[j], ev_w_out)
        else:
            h = _odd_layer(h, seq, cs, norm_mix[layer], j, od_w_in[j], od_q_norm[j],
                           od_kv_norm[j], od_w_uq[j], od_w_ukv[j], od_w_o)
        act = ffn_up(h, norm_ffn[layer][None, :], ff_w_up, ff_conv_w, ff_conv_b3, layer, seq)
        h = matmul_res(act, w_down_bf16, layer, h, 512)
    return rmsnorm(h, norm_final[None, :]).reshape(bsz, seq, d)
```
